```python
import math
import jax, jax.numpy as jnp
from jax import lax
import numpy as np

D_MODEL = 1024
BATCH = 32
SEQ = 2048
DEPTH = 1

N_MEM = 256
ATTN_HEADS = 8
HEAD_DIM = 64
D_ATTN = ATTN_HEADS * HEAD_DIM
D_CONV = D_MODEL - D_ATTN
CONV_WIDTH = 31
Q_BLOCK = 128
MEM_HEADS = 4
MEM_HEAD_DIM = D_MODEL // MEM_HEADS
N_GROUPS = 4
EXPERTS_PER_GROUP = 8
N_EXPERTS = N_GROUPS * EXPERTS_PER_GROUP
TOP_K_IN_GROUP = 2
D_EXPERT = D_MODEL // 2
EPS = 1e-6
OFF_Q = 0
OFF_K = OFF_Q + D_ATTN
OFF_V = OFF_K + D_ATTN
OFF_F = OFF_V + D_ATTN
OFF_C = OFF_F + ATTN_HEADS
D_IN = OFF_C + 2 * D_CONV

kernel_name = "hybrid_fox_conformer_hmoe_block"


def rmsnorm(x, g):
    xf = x.astype(jnp.float32)
    y = xf * lax.rsqrt(jnp.mean(xf * xf, axis=-1, keepdims=True) + EPS)
    return (y * g.astype(jnp.float32)).astype(x.dtype)


def layernorm(x, g, b):
    xf = x.astype(jnp.float32)
    mu = jnp.mean(xf, axis=-1, keepdims=True)
    xc = xf - mu
    y = xc * lax.rsqrt(jnp.mean(xc * xc, axis=-1, keepdims=True) + EPS)
    return (y * g.astype(jnp.float32) + b.astype(jnp.float32)).astype(x.dtype)


def forgetting_attention(q, k, v, log_f):
    S = q.shape[2]
    c = jnp.cumsum(log_f, axis=-1)
    scale = 1.0 / math.sqrt(HEAD_DIM)
    outs = []
    for i in range(S // Q_BLOCK):
        lo, hi = i * Q_BLOCK, (i + 1) * Q_BLOCK
        s = jnp.einsum('bhqd,bhkd->bhqk', q[:, :, lo:hi], k[:, :, :hi]).astype(jnp.float32) * scale
        s = s + c[:, :, lo:hi, None] - c[:, :, None, :hi]
        causal = (lo + jnp.arange(Q_BLOCK))[:, None] >= jnp.arange(hi)[None, :]
        s = jnp.where(causal, s, -jnp.inf)
        p = jax.nn.softmax(s, axis=-1).astype(v.dtype)
        outs.append(jnp.einsum('bhqk,bhkd->bhqd', p, v[:, :, :hi]))
    return jnp.concatenate(outs, axis=2)


def conformer_conv(u, b_glu, w_dw, b_dw, ln_g, ln_b):
    u = u + b_glu
    a, gate = jnp.split(u, 2, axis=-1)
    z = a * jax.nn.sigmoid(gate)
    zp = jnp.pad(z, ((0, 0), (CONV_WIDTH - 1, 0), (0, 0)))
    z = lax.conv_general_dilated(zp, w_dw[:, None, :], window_strides=(1,), padding='VALID',
                                 dimension_numbers=('NWC', 'WIO', 'NWC'),
                                 feature_group_count=D_CONV) + b_dw
    z = layernorm(z, ln_g, ln_b)
    return jax.nn.silu(z)


def parallel_mixer(h, w_in, b_forget, b_glu, w_dw, b_dw, conv_ln_g, conv_ln_b, attn_out_g, w_out):
    B, S, _ = h.shape
    proj = h @ w_in
    def heads(t):
        return t.reshape(B, S, ATTN_HEADS, HEAD_DIM).transpose(0, 2, 1, 3)
    q = heads(proj[..., OFF_Q:OFF_K])
    k = heads(proj[..., OFF_K:OFF_V])
    v = heads(proj[..., OFF_V:OFF_F])
    log_f = jax.nn.log_sigmoid((proj[..., OFF_F:OFF_C] + b_forget).astype(jnp.float32))
    log_f = log_f.transpose(0, 2, 1)
    attn = forgetting_attention(q, k, v, log_f)
    attn = attn.transpose(0, 2, 1, 3).reshape(B, S, D_ATTN)
    attn = rmsnorm(attn, attn_out_g)
    conv = conformer_conv(proj[..., OFF_C:], b_glu, w_dw, b_dw, conv_ln_g, conv_ln_b)
    return jnp.concatenate([attn, conv], axis=-1) @ w_out


def memory_cross_attention(h, mem_n, w_mq, w_mkv, w_mo):
    B, S, _ = h.shape
    q = (h @ w_mq).reshape(B, S, MEM_HEADS, MEM_HEAD_DIM)
    kv = mem_n @ w_mkv
    k, v = jnp.split(kv, 2, axis=-1)
    k = k.reshape(B, -1, MEM_HEADS, MEM_HEAD_DIM)
    v = v.reshape(B, -1, MEM_HEADS, MEM_HEAD_DIM)
    s = jnp.einsum('bshd,bmhd->bhsm', q, k).astype(jnp.float32) / math.sqrt(MEM_HEAD_DIM)
    p = jax.nn.softmax(s, axis=-1).astype(v.dtype)
    o = jnp.einsum('bhsm,bmhd->bshd', p, v).reshape(B, S, D_MODEL)
    return o @ w_mo


def hierarchical_moe(h, w_route_group, b_route_group, w_route_expert, b_route_expert,
                     w_gate, w_up, w_down):
    B, S, D = h.shape
    t = h.reshape(-1, D)
    p_group = jax.nn.softmax((t @ w_route_group).astype(jnp.float32) + b_route_group, axis=-1)
    g_top, g_idx = lax.top_k(p_group, 1)
    e_logits = jnp.einsum('td,gde->tge', t, w_route_expert).astype(jnp.float32) + b_route_expert
    e_logits = jnp.take_along_axis(e_logits, g_idx[:, :, None], axis=1)[:, 0]
    p_exp = jax.nn.softmax(e_logits, axis=-1)
    e_top, e_idx = lax.top_k(p_exp, TOP_K_IN_GROUP)
    w = g_top * e_top / jnp.sum(e_top, axis=-1, keepdims=True)
    ids = g_idx * EXPERTS_PER_GROUP + e_idx
    gates = jnp.sum(jax.nn.one_hot(ids, N_EXPERTS, dtype=jnp.float32) * w[..., None], axis=1)
    gates = gates.astype(h.dtype).T

    def expert_step(y, args):
        wg, wu, wd, ge = args
        out = (jax.nn.silu(t @ wg) * (t @ wu)) @ wd
        return y + ge[:, None] * out, None

    y, _ = lax.scan(expert_step, jnp.zeros_like(t), (w_gate, w_up, w_down, gates))
    return y.reshape(B, S, D)


def setup_inputs(seed: int = 0) -> dict:
    key = jax.random.key(seed)
    ks = iter(jax.random.split(key, 32))
    f32 = jnp.float32

    def nrm(shape, fan_in):
        return jax.random.normal(next(ks), shape, f32) * (fan_in ** -0.5)

    def gain(shape):
        return 1.0 + 0.02 * jax.random.normal(next(ks), shape, f32)

    def small(shape):
        return 0.01 * jax.random.normal(next(ks), shape, f32)

    L = DEPTH
    return {
        "x": jax.random.normal(next(ks), (BATCH, SEQ, D_MODEL), f32),
        "mem": jax.random.normal(next(ks), (BATCH, N_MEM, D_MODEL), f32),
        "norm_mix_g": gain((L, D_MODEL)),
        "w_in": nrm((L, D_MODEL, D_IN), D_MODEL),
        "b_forget": jax.random.uniform(next(ks), (L, ATTN_HEADS), f32, 1.0, 6.0),
        "b_glu": small((L, 2 * D_CONV)),
        "w_dw": nrm((L, CONV_WIDTH, D_CONV), CONV_WIDTH),
        "b_dw": small((L, D_CONV)),
        "conv_ln_g": gain((L, D_CONV)),
        "conv_ln_b": small((L, D_CONV)),
        "attn_out_g": gain((L, D_ATTN)),
        "w_out": nrm((L, D_ATTN + D_CONV, D_MODEL), D_ATTN + D_CONV),
        "norm_mem_g": gain((L, D_MODEL)),
        "mem_norm_g": gain((L, D_MODEL)),
        "w_mq": nrm((L, D_MODEL, D_MODEL), D_MODEL),
        "w_mkv": nrm((L, D_MODEL, 2 * D_MODEL), D_MODEL),
        "w_mo": nrm((L, D_MODEL, D_MODEL), D_MODEL),
        "norm_ffn_g": gain((L, D_MODEL)),
        "w_route_group": nrm((L, D_MODEL, N_GROUPS), D_MODEL),
        "b_route_group": small((L, N_GROUPS)),
        "w_route_expert": nrm((L, N_GROUPS, D_MODEL, EXPERTS_PER_GROUP), D_MODEL),
        "b_route_expert": small((L, N_GROUPS, EXPERTS_PER_GROUP)),
        "w_gate": nrm((L, N_EXPERTS, D_MODEL, D_EXPERT), D_MODEL),
        "w_up": nrm((L, N_EXPERTS, D_MODEL, D_EXPERT), D_MODEL),
        "w_down": nrm((L, N_EXPERTS, D_EXPERT, D_MODEL), D_EXPERT),
        "final_g": gain((D_MODEL,)),
    }


def reference(x, mem, norm_mix_g, w_in, b_forget, b_glu, w_dw, b_dw, conv_ln_g, conv_ln_b,
              attn_out_g, w_out, norm_mem_g, mem_norm_g, w_mq, w_mkv, w_mo, norm_ffn_g,
              w_route_group, b_route_group, w_route_expert, b_route_expert,
              w_gate, w_up, w_down, final_g):
    for l in range(DEPTH):
        h = rmsnorm(x, norm_mix_g[l])
        x = x + parallel_mixer(h, w_in[l], b_forget[l], b_glu[l], w_dw[l], b_dw[l],
                               conv_ln_g[l], conv_ln_b[l], attn_out_g[l], w_out[l])
        h = rmsnorm(x, norm_mem_g[l])
        mem_n = rmsnorm(mem, mem_norm_g[l])
        x = x + memory_cross_attention(h, mem_n, w_mq[l], w_mkv[l], w_mo[l])
        h = rmsnorm(x, norm_ffn_g[l])
        x = x + hierarchical_moe(h, w_route_group[l], b_route_group[l], w_route_expert[l],
                                 b_route_expert[l], w_gate[l], w_up[l], w_down[l])
    return rmsnorm(x, final_g)
```

```python
import functools

import jax
import jax.numpy as jnp
from jax import lax
from jax.experimental import pallas as pl
from jax.experimental.pallas import tpu as pltpu

F32 = jnp.float32
BF16 = jnp.bfloat16

D_MODEL = 1024
ATTN_HEADS = 8
HEAD_DIM = 64
D_ATTN = ATTN_HEADS * HEAD_DIM
D_CONV = D_MODEL - D_ATTN
CONV_WIDTH = 31
MEM_HEADS = 4
MEM_HEAD_DIM = D_MODEL // MEM_HEADS
N_GROUPS = 4
EXPERTS_PER_GROUP = 8
N_EXPERTS = N_GROUPS * EXPERTS_PER_GROUP
D_EXPERT = D_MODEL // 2
EPS = 1e-6

LANES = 128
HEAD_PAIRS = ATTN_HEADS // 2
COL_Q = 0
COL_K = COL_Q + D_ATTN
COL_V = COL_K + D_ATTN
COL_F = COL_V + D_ATTN
COL_C = COL_F + LANES
D_IN_PACKED = COL_C + 2 * D_CONV
ROUTE_OFF = N_GROUPS
NEG_BIG = -1e30
VMEM_LIMIT = 48 * 1024 * 1024

TM_IN = 512
TQ = 256
TS_CONV = 256
CONV_HALO = 32
CONV_ROWS = 32
TM_MID = 512
TM_MOE = 1024


def _rms(x, g):
    return x * lax.rsqrt(jnp.mean(x * x, axis=-1, keepdims=True) + EPS) * g


def _split3(c):
    hi = c.astype(BF16)
    r = c - hi.astype(F32)
    mid = r.astype(BF16)
    lo = (r - mid.astype(F32)).astype(BF16)
    return hi, mid, lo


def _params(*sem):
    return pltpu.CompilerParams(dimension_semantics=sem, vmem_limit_bytes=VMEM_LIMIT)


def _inproj_kernel(x_ref, g_ref, w_ref, bf_ref, bglu_ref, tri_ref,
                   q_ref, k_ref, v_ref, c_ref, z_ref, carry_ref):
    @pl.when(pl.program_id(1) == 0)
    def _():
        carry_ref[...] = jnp.zeros_like(carry_ref)

    h = _rms(x_ref[...], g_ref[...]).astype(BF16)
    q_ref[...] = jnp.dot(h, w_ref[:, COL_Q:COL_K], preferred_element_type=F32).astype(BF16)
    k_ref[...] = jnp.dot(h, w_ref[:, COL_K:COL_V], preferred_element_type=F32).astype(BF16)
    v_ref[...] = jnp.dot(h, w_ref[:, COL_V:COL_F], preferred_element_type=F32).astype(BF16)

    fl = jnp.dot(h, w_ref[:, COL_F:COL_C], preferred_element_type=F32) + bf_ref[...]
    lf = jnp.minimum(fl, 0.0) - jnp.log(1.0 + jnp.exp(-jnp.abs(fl)))
    tri = tri_ref[...]
    c = carry_ref[...]
    for part in _split3(lf):
        c = c + jnp.dot(tri, part, preferred_element_type=F32)
    carry_ref[...] = c[-1:, :]
    c_ref[...] = c[:, :ATTN_HEADS]

    u = jnp.dot(h, w_ref[:, COL_C:D_IN_PACKED], preferred_element_type=F32) + bglu_ref[...]
    z_ref[...] = (u[:, :D_CONV] * jax.nn.sigmoid(u[:, D_CONV:])).astype(BF16)


def _inproj(x2d, g, w_packed, bf_pad, b_glu, batch, seq):
    t = batch * seq
    nt = seq // TM_IN
    tri = jnp.tril(jnp.ones((TM_IN, TM_IN), BF16))
    tok = lambda b, i: (b * nt + i, 0)
    const = lambda b, i: (0, 0)
    return pl.pallas_call(
        _inproj_kernel,
        grid=(batch, nt),
        in_specs=[
            pl.BlockSpec((TM_IN, D_MODEL), tok),
            pl.BlockSpec((1, D_MODEL), const),
            pl.BlockSpec((D_MODEL, D_IN_PACKED), const),
            pl.BlockSpec((1, LANES), const),
            pl.BlockSpec((1, 2 * D_CONV), const),
            pl.BlockSpec((TM_IN, TM_IN), const),
        ],
        out_specs=[
            pl.BlockSpec((TM_IN, D_ATTN), tok),
            pl.BlockSpec((TM_IN, D_ATTN), tok),
            pl.BlockSpec((TM_IN, D_ATTN), tok),
            pl.BlockSpec((TM_IN, ATTN_HEADS), tok),
            pl.BlockSpec((TM_IN, D_CONV), tok),
        ],
        out_shape=[
            jax.ShapeDtypeStruct((t, D_ATTN), BF16),
            jax.ShapeDtypeStruct((t, D_ATTN), BF16),
            jax.ShapeDtypeStruct((t, D_ATTN), BF16),
            jax.ShapeDtypeStruct((t, ATTN_HEADS), F32),
            jax.ShapeDtypeStruct((t, D_CONV), BF16),
        ],
        scratch_shapes=[pltpu.VMEM((1, LANES), F32)],
        compiler_params=_params("arbitrary", "arbitrary"),
        name="inproj",
    )(x2d, g, w_packed, bf_pad, b_glu, tri)


def _attn_kernel(q_ref, k_ref, v_ref, c_ref, o_ref, ka0, ka1, va0, va1):
    i = pl.program_id(2)
    ka, va = (ka0, ka1), (va0, va1)

    def halves(rows):
        lane = lax.broadcasted_iota(jnp.int32, (rows, LANES), 1)
        own = (lane < HEAD_DIM, lane >= HEAD_DIM)
        aug = (lane - HEAD_DIM, lane)
        return own, aug

    def split3_f32(c):
        return [p.astype(F32) for p in _split3(c)]

    @pl.when(i == 0)
    def _():
        own, aug = halves(k_ref.shape[0])
        k = k_ref[...].astype(F32)
        v = v_ref[...].astype(F32)
        c = c_ref[...]
        for h in range(2):
            a = aug[h]
            hi, mid, lo = split3_f32(c[:, h:h + 1])
            extra = jnp.where(a < 3, 1.0, jnp.where(a == 3, -hi, jnp.where(a == 4, -mid, jnp.where(a == 5, -lo, 0.0))))
            ka[h][...] = jnp.where(own[h], k, extra).astype(BF16)
            va[h][...] = jnp.where(own[h], v, 1.0).astype(BF16)

    row0 = pl.multiple_of(i * TQ, TQ)
    own, aug = halves(TQ)
    q = q_ref[...].astype(F32) * (HEAD_DIM ** -0.5)
    cq = c_ref[pl.ds(row0, TQ), :]
    qa = []
    for h in range(2):
        a = aug[h]
        hi, mid, lo = split3_f32(cq[:, h:h + 1])
        extra = jnp.where(a == 0, hi, jnp.where(a == 1, mid, jnp.where(a == 2, lo, jnp.where(a < 6, 1.0, 0.0))))
        qa.append(jnp.where(own[h], q, extra).astype(BF16))

    def chunk(j, carry, masked):
        k0 = pl.multiple_of(j * TQ, TQ)
        out = []
        for h in range(2):
            m, acc = carry[2 * h], carry[2 * h + 1]
            s = lax.dot_general(qa[h], ka[h][pl.ds(k0, TQ), :], (((1,), (1,)), ((), ())),
                                preferred_element_type=F32)
            if masked:
                r = lax.broadcasted_iota(jnp.int32, (TQ, TQ), 0)
                cc = lax.broadcasted_iota(jnp.int32, (TQ, TQ), 1)
                s = jnp.where(r >= cc, s, NEG_BIG)
            m_new = jnp.maximum(m, jnp.max(s, axis=-1, keepdims=True))
            alpha = jnp.exp(m - m_new)
            p = jnp.exp(s - m_new).astype(BF16)
            acc = alpha * acc + jnp.dot(p, va[h][pl.ds(k0, TQ), :], preferred_element_type=F32)
            out += [m_new, acc]
        return tuple(out)

    init = (jnp.full((TQ, 1), NEG_BIG, F32), jnp.zeros((TQ, LANES), F32)) * 2
    carry = lax.fori_loop(0, i, lambda j, c: chunk(j, c, False), init)
    _, acc0, _, acc1 = chunk(i, carry, True)
    o0 = acc0 / pltpu.roll(acc0, HEAD_DIM, 1)
    o1 = acc1 / pltpu.roll(acc1, HEAD_DIM, 1)
    o_ref[...] = jnp.where(own[0], o0, o1).astype(BF16)


def _fox_attn(q, k, v, c4, batch, seq):
    t = batch * seq
    nq = seq // TQ
    return pl.pallas_call(
        _attn_kernel,
        grid=(batch, HEAD_PAIRS, nq),
        in_specs=[
            pl.BlockSpec((TQ, LANES), lambda b, p, i: (b * nq + i, p)),
            pl.BlockSpec((seq, LANES), lambda b, p, i: (b, p)),
            pl.BlockSpec((seq, LANES), lambda b, p, i: (b, p)),
            pl.BlockSpec((None, None, seq, 2), lambda b, p, i: (p, b, 0, 0)),
        ],
        out_specs=pl.BlockSpec((TQ, LANES), lambda b, p, i: (b * nq + i, p)),
        out_shape=jax.ShapeDtypeStruct((t, D_ATTN), BF16),
        scratch_shapes=[pltpu.VMEM((seq, LANES), BF16)] * 4,
        compiler_params=_params("arbitrary", "arbitrary", "arbitrary"),
        name="fox_attn",
    )(q, k, v, c4)


def _conv_kernel(z_ref, w_ref, b_ref, g_ref, beta_ref, o_ref, buf_ref):
    @pl.when(pl.program_id(1) == 0)
    def _():
        buf_ref[0:CONV_HALO, :] = jnp.zeros((CONV_HALO, D_CONV), F32)

    buf_ref[CONV_HALO:, :] = z_ref[...].astype(F32)
    first = CONV_HALO - (CONV_WIDTH - 1)
    for r in range(TS_CONV // CONV_ROWS):
        acc = jnp.zeros((CONV_ROWS, D_CONV), F32)
        for j in range(CONV_WIDTH):
            lo = r * CONV_ROWS + first + j
            acc = acc + buf_ref[lo:lo + CONV_ROWS, :] * w_ref[j:j + 1, :]
        y = acc + b_ref[...]
        mu = jnp.mean(y, axis=-1, keepdims=True)
        yc = y - mu
        y = yc * lax.rsqrt(jnp.mean(yc * yc, axis=-1, keepdims=True) + EPS) * g_ref[...] + beta_ref[...]
        o_ref[r * CONV_ROWS:(r + 1) * CONV_ROWS, :] = (y * jax.nn.sigmoid(y)).astype(BF16)
    buf_ref[0:CONV_HALO, :] = buf_ref[TS_CONV:TS_CONV + CONV_HALO, :]


def _conv(z, w_dw_pad, b_dw, ln_g, ln_b, batch, seq):
    t = batch * seq
    ns = seq // TS_CONV
    tok = lambda b, i: (b * ns + i, 0)
    const = lambda b, i: (0, 0)
    return pl.pallas_call(
        _conv_kernel,
        grid=(batch, ns),
        in_specs=[
            pl.BlockSpec((TS_CONV, D_CONV), tok),
            pl.BlockSpec((CONV_HALO, D_CONV), const),
            pl.BlockSpec((1, D_CONV), const),
            pl.BlockSpec((1, D_CONV), const),
            pl.BlockSpec((1, D_CONV), const),
        ],
        out_specs=pl.BlockSpec((TS_CONV, D_CONV), tok),
        out_shape=jax.ShapeDtypeStruct((t, D_CONV), BF16),
        scratch_shapes=[pltpu.VMEM((TS_CONV + CONV_HALO, D_CONV), F32)],
        compiler_params=_params("arbitrary", "arbitrary"),
        name="conv",
    )(z, w_dw_pad, b_dw, ln_g, ln_b)


def _memkv_kernel(m_ref, g_ref, w_ref, k_ref, v_ref):
    h = _rms(m_ref[...], g_ref[...]).astype(BF16)
    k = jnp.dot(h, w_ref[:, :D_MODEL], preferred_element_type=F32)
    k_ref[...] = (k * (MEM_HEAD_DIM ** -0.5)).astype(BF16)
    v_ref[...] = jnp.dot(h, w_ref[:, D_MODEL:], preferred_element_type=F32).astype(BF16)


def _mem_kv(mem2d, g, w_mkv, batch, n_mem):
    return pl.pallas_call(
        _memkv_kernel,
        grid=(batch,),
        in_specs=[
            pl.BlockSpec((n_mem, D_MODEL), lambda b: (b, 0)),
            pl.BlockSpec((1, D_MODEL), lambda b: (0, 0)),
            pl.BlockSpec((D_MODEL, 2 * D_MODEL), lambda b: (0, 0)),
        ],
        out_specs=[pl.BlockSpec((n_mem, D_MODEL), lambda b: (b, 0))] * 2,
        out_shape=[jax.ShapeDtypeStruct((batch * n_mem, D_MODEL), BF16)] * 2,
        compiler_params=_params("arbitrary"),
        name="mem_kv",
    )(mem2d, g, w_mkv)


def _mid_kernel(x_ref, a_ref, cv_ref, ag_ref, wout_ref, gm_ref, wmq_ref, km_ref, vm_ref, wmo_ref,
                gf_ref, wr_ref, br_ref,
                x2_ref, h3_ref, gates_ref):
    a = _rms(a_ref[...].astype(F32), ag_ref[...]).astype(BF16)
    mix = jnp.dot(a, wout_ref[:D_ATTN, :], preferred_element_type=F32)
    mix = mix + jnp.dot(cv_ref[...], wout_ref[D_ATTN:, :], preferred_element_type=F32)
    x1 = x_ref[...] + mix

    h2 = _rms(x1, gm_ref[...]).astype(BF16)
    qm = jnp.dot(h2, wmq_ref[...], preferred_element_type=F32).astype(BF16)
    heads = []
    for hh in range(MEM_HEADS):
        sl = slice(hh * MEM_HEAD_DIM, (hh + 1) * MEM_HEAD_DIM)
        s = lax.dot_general(qm[:, sl], km_ref[:, sl], (((1,), (1,)), ((), ())), preferred_element_type=F32)
        p = jnp.exp(s - jnp.max(s, axis=-1, keepdims=True))
        l = jnp.sum(p, axis=-1, keepdims=True)
        o = jnp.dot(p.astype(BF16), vm_ref[:, sl], preferred_element_type=F32) / l
        heads.append(o.astype(BF16))
    o = jnp.concatenate(heads, axis=-1)
    x2 = x1 + jnp.dot(o, wmo_ref[...], preferred_element_type=F32)
    x2_ref[...] = x2

    h3 = _rms(x2, gf_ref[...])
    h3_ref[...] = h3.astype(BF16)
    logits = jnp.dot(h3, wr_ref[...], preferred_element_type=F32, precision=lax.Precision.HIGHEST) + br_ref[...]
    lane = lax.broadcasted_iota(jnp.int32, logits.shape, 1)

    def top(vals):
        m = jnp.max(vals, axis=-1, keepdims=True)
        idx = jnp.min(jnp.where(vals == m, lane, LANES), axis=-1, keepdims=True)
        return m, idx

    gl = jnp.where(lane < N_GROUPS, logits, NEG_BIG)
    gmax, gidx = top(gl)
    g_top = 1.0 / jnp.sum(jnp.exp(gl - gmax), axis=-1, keepdims=True)
    first = ROUTE_OFF + EXPERTS_PER_GROUP * gidx
    el = jnp.where((lane >= first) & (lane < first + EXPERTS_PER_GROUP), logits, NEG_BIG)
    m1, i1 = top(el)
    m2, i2 = top(jnp.where(lane == i1, NEG_BIG, el))
    e2 = jnp.exp(m2 - m1)
    w1 = g_top / (1.0 + e2)
    w2 = g_top * e2 / (1.0 + e2)
    gates_ref[...] = jnp.where(lane == i1, w1, jnp.where(lane == i2, w2, 0.0))


def _mid(x2d, attn, conv, ag, w_out, gm, w_mq, kmem, vmem, w_mo, gf, w_route, b_route, batch, seq, n_mem):
    t = batch * seq
    per_b = seq // TM_MID
    tok = lambda i: (i, 0)
    const = lambda i: (0, 0)
    memb = lambda i: (i // per_b, 0)
    return pl.pallas_call(
        _mid_kernel,
        grid=(t // TM_MID,),
        in_specs=[
            pl.BlockSpec((TM_MID, D_MODEL), tok),
            pl.BlockSpec((TM_MID, D_ATTN), tok),
            pl.BlockSpec((TM_MID, D_CONV), tok),
            pl.BlockSpec((1, D_ATTN), const),
            pl.BlockSpec((D_MODEL, D_MODEL), const),
            pl.BlockSpec((1, D_MODEL), const),
            pl.BlockSpec((D_MODEL, D_MODEL), const),
            pl.BlockSpec((n_mem, D_MODEL), memb),
            pl.BlockSpec((n_mem, D_MODEL), memb),
            pl.BlockSpec((D_MODEL, D_MODEL), const),
            pl.BlockSpec((1, D_MODEL), const),
            pl.BlockSpec((D_MODEL, LANES), const),
            pl.BlockSpec((1, LANES), const),
        ],
        out_specs=[
            pl.BlockSpec((TM_MID, D_MODEL), tok),
            pl.BlockSpec((TM_MID, D_MODEL), tok),
            pl.BlockSpec((TM_MID, LANES), tok),
        ],
        out_shape=[
            jax.ShapeDtypeStruct((t, D_MODEL), F32),
            jax.ShapeDtypeStruct((t, D_MODEL), BF16),
            jax.ShapeDtypeStruct((t, LANES), F32),
        ],
        compiler_params=_params("arbitrary"),
        name="mid",
    )(x2d, attn, conv, ag, w_out, gm, w_mq, kmem, vmem, w_mo, gf, w_route, b_route)


def _moe_kernel(h_ref, gates_ref, wg_ref, wu_ref, wd_ref, x2_ref, fg_ref, o_ref, acc_ref):
    e = pl.program_id(1)

    @pl.when(e == 0)
    def _():
        acc_ref[...] = jnp.zeros_like(acc_ref)

    h = h_ref[...]
    hg = jnp.dot(h, wg_ref[...], preferred_element_type=F32)
    hu = jnp.dot(h, wu_ref[...], preferred_element_type=F32)
    act = (hg * jax.nn.sigmoid(hg) * hu).astype(BF16)
    y = jnp.dot(act, wd_ref[...], preferred_element_type=F32)
    gates = gates_ref[...]
    lane = lax.broadcasted_iota(jnp.int32, gates.shape, 1)
    gate = jnp.sum(jnp.where(lane == e + ROUTE_OFF, gates, 0.0), axis=-1, keepdims=True)
    acc_ref[...] += gate * y

    @pl.when(e == N_EXPERTS - 1)
    def _():
        o_ref[...] = _rms(x2_ref[...] + acc_ref[...], fg_ref[...])


def _moe(h3, gates, w_gate, w_up, w_down, x2, final_g):
    t = h3.shape[0]
    tok = lambda i, e: (i, 0)
    return pl.pallas_call(
        _moe_kernel,
        grid=(t // TM_MOE, N_EXPERTS),
        in_specs=[
            pl.BlockSpec((TM_MOE, D_MODEL), tok),
            pl.BlockSpec((TM_MOE, LANES), tok),
            pl.BlockSpec((None, D_MODEL, D_EXPERT), lambda i, e: (e, 0, 0)),
            pl.BlockSpec((None, D_MODEL, D_EXPERT), lambda i, e: (e, 0, 0)),
            pl.BlockSpec((None, D_EXPERT, D_MODEL), lambda i, e: (e, 0, 0)),
            pl.BlockSpec((TM_MOE, D_MODEL), tok),
            pl.BlockSpec((1, D_MODEL), lambda i, e: (0, 0)),
        ],
        out_specs=pl.BlockSpec((TM_MOE, D_MODEL), tok),
        out_shape=jax.ShapeDtypeStruct((t, D_MODEL), F32),
        scratch_shapes=[pltpu.VMEM((TM_MOE, D_MODEL), F32)],
        compiler_params=_params("arbitrary", "arbitrary"),
        name="moe",
    )(h3, gates, w_gate, w_up, w_down, x2, final_g)


def kernel(x, mem, norm_mix_g, w_in, b_forget, b_glu, w_dw, b_dw, conv_ln_g, conv_ln_b, attn_out_g, w_out,
           norm_mem_g, mem_norm_g, w_mq, w_mkv, w_mo, norm_ffn_g, w_route_group, b_route_group,
           w_route_expert, b_route_expert, w_gate, w_up, w_down, final_g):
    batch, seq, _ = x.shape
    n_mem = mem.shape[1]
    depth = norm_mix_g.shape[0]
    xs = x.reshape(batch * seq, D_MODEL)
    mem2d = mem.reshape(batch * n_mem, D_MODEL)
    row = lambda v: v.reshape(1, -1)

    for l in range(depth):
        wi = w_in[l]
        w_packed = jnp.concatenate(
            [wi[:, :3 * D_ATTN], jnp.pad(wi[:, 3 * D_ATTN:3 * D_ATTN + ATTN_HEADS], ((0, 0), (0, LANES - ATTN_HEADS))),
             wi[:, 3 * D_ATTN + ATTN_HEADS:]], axis=1).astype(BF16)
        bf_pad = jnp.pad(b_forget[l], (0, LANES - ATTN_HEADS)).reshape(1, LANES)
        q, k, v, c, z = _inproj(xs, row(norm_mix_g[l]), w_packed, bf_pad, row(b_glu[l]), batch, seq)
        c4 = c.reshape(batch, seq, HEAD_PAIRS, 2).transpose(2, 0, 1, 3)
        attn = _fox_attn(q, k, v, c4, batch, seq)
        w_dw_pad = jnp.pad(w_dw[l], ((0, CONV_HALO - CONV_WIDTH), (0, 0)))
        conv = _conv(z, w_dw_pad, row(b_dw[l]), row(conv_ln_g[l]), row(conv_ln_b[l]), batch, seq)
        kmem, vmem = _mem_kv(mem2d, row(mem_norm_g[l]), w_mkv[l].astype(BF16), batch, n_mem)
        w_route = jnp.concatenate(
            [w_route_group[l], w_route_expert[l].transpose(1, 0, 2).reshape(D_MODEL, N_EXPERTS)], axis=1)
        w_route = jnp.pad(w_route, ((0, 0), (0, LANES - N_GROUPS - N_EXPERTS)))
        b_route = jnp.pad(jnp.concatenate([b_route_group[l], b_route_expert[l].reshape(-1)]),
                          (0, LANES - N_GROUPS - N_EXPERTS)).reshape(1, LANES)
        x2, h3, gates = _mid(xs, attn, conv, row(attn_out_g[l]), w_out[l].astype(BF16), row(norm_mem_g[l]),
                             w_mq[l].astype(BF16), kmem, vmem, w_mo[l].astype(BF16), row(norm_ffn_g[l]),
                             w_route, b_route, batch, seq, n_mem)
        last = l == depth - 1
        fg = row(final_g) if last else jnp.ones((1, D_MODEL), F32)
        xs = _moe(h3, gates, w_gate[l].astype(BF16), w_up[l].astype(BF16), w_down[l].astype(BF16), x2, fg)
    return xs.reshape(batch, seq, D_MODEL)
```

```python
import functools

import jax
import jax.numpy as jnp
import numpy as np
from jax import lax
from jax.experimental import pallas as pl
from jax.experimental.pallas import tpu as pltpu

F32 = jnp.float32
BF16 = jnp.bfloat16
I32 = jnp.int32

D_MODEL = 1024
ATTN_HEADS = 8
HEAD_DIM = 64
D_ATTN = ATTN_HEADS * HEAD_DIM
D_CONV = D_MODEL - D_ATTN
CONV_WIDTH = 31
MEM_HEADS = 4
MEM_HEAD_DIM = D_MODEL // MEM_HEADS
N_GROUPS = 4
EXPERTS_PER_GROUP = 8
N_EXPERTS = N_GROUPS * EXPERTS_PER_GROUP
D_EXPERT = D_MODEL // 2
EPS = 1e-6

LANES = 128
SUBLANES = 8
HEAD_PAIRS = ATTN_HEADS // 2
COL_Q = 0
COL_K = COL_Q + D_ATTN
COL_V = COL_K + D_ATTN
COL_F = COL_V + D_ATTN
COL_C = COL_F + LANES
D_IN_PACKED = COL_C + 2 * D_CONV
ROUTE_OFF = N_GROUPS
PAIRS_PER_GROUP = EXPERTS_PER_GROUP * (EXPERTS_PER_GROUP - 1) // 2
N_BUCKETS = N_GROUPS * PAIRS_PER_GROUP
D_ROW = D_MODEL + LANES
NEG_BIG = -1e30
VMEM_LIMIT = 48 * 1024 * 1024

TM_IN = 512
TQ = 256
TS_CONV = 256
CONV_HALO = 32
CONV_ROWS = 32
TM_MID = 512
TM_PLAN = 2048
TM_DISP = 1024
TMX = 256
TM_COMB = 512
DMA_UNROLL = 8

assert N_BUCKETS <= LANES


def _rms(x, g):
    return x * lax.rsqrt(jnp.mean(x * x, axis=-1, keepdims=True) + EPS) * g


def _split3(c):
    hi = c.astype(BF16)
    r = c - hi.astype(F32)
    mid = r.astype(BF16)
    lo = (r - mid.astype(F32)).astype(BF16)
    return hi, mid, lo


def _params(*sem):
    return pltpu.CompilerParams(dimension_semantics=sem, vmem_limit_bytes=VMEM_LIMIT)


def _inproj_kernel(x_ref, g_ref, w_ref, bf_ref, bglu_ref, tri_ref,
                   q_ref, k_ref, v_ref, c_ref, z_ref, carry_ref):
    @pl.when(pl.program_id(1) == 0)
    def _():
        carry_ref[...] = jnp.zeros_like(carry_ref)

    h = _rms(x_ref[...], g_ref[...]).astype(BF16)
    q_ref[...] = jnp.dot(h, w_ref[:, COL_Q:COL_K], preferred_element_type=F32).astype(BF16)
    k_ref[...] = jnp.dot(h, w_ref[:, COL_K:COL_V], preferred_element_type=F32).astype(BF16)
    v_ref[...] = jnp.dot(h, w_ref[:, COL_V:COL_F], preferred_element_type=F32).astype(BF16)

    fl = jnp.dot(h, w_ref[:, COL_F:COL_C], preferred_element_type=F32) + bf_ref[...]
    lf = jnp.minimum(fl, 0.0) - jnp.log(1.0 + jnp.exp(-jnp.abs(fl)))
    tri = tri_ref[...]
    c = carry_ref[...]
    for part in _split3(lf):
        c = c + jnp.dot(tri, part, preferred_element_type=F32)
    carry_ref[...] = c[-1:, :]
    c_ref[...] = c[:, :ATTN_HEADS]

    u = jnp.dot(h, w_ref[:, COL_C:D_IN_PACKED], preferred_element_type=F32) + bglu_ref[...]
    z_ref[...] = (u[:, :D_CONV] * jax.nn.sigmoid(u[:, D_CONV:])).astype(BF16)


def _inproj(x2d, g, w_packed, bf_pad, b_glu, batch, seq):
    t = batch * seq
    nt = seq // TM_IN
    tri = jnp.tril(jnp.ones((TM_IN, TM_IN), BF16))
    tok = lambda b, i: (b * nt + i, 0)
    const = lambda b, i: (0, 0)
    return pl.pallas_call(
        _inproj_kernel,
        grid=(batch, nt),
        in_specs=[
            pl.BlockSpec((TM_IN, D_MODEL), tok),
            pl.BlockSpec((1, D_MODEL), const),
            pl.BlockSpec((D_MODEL, D_IN_PACKED), const),
            pl.BlockSpec((1, LANES), const),
            pl.BlockSpec((1, 2 * D_CONV), const),
            pl.BlockSpec((TM_IN, TM_IN), const),
        ],
        out_specs=[
            pl.BlockSpec((TM_IN, D_ATTN), tok),
            pl.BlockSpec((TM_IN, D_ATTN), tok),
            pl.BlockSpec((TM_IN, D_ATTN), tok),
            pl.BlockSpec((TM_IN, ATTN_HEADS), tok),
            pl.BlockSpec((TM_IN, D_CONV), tok),
        ],
        out_shape=[
            jax.ShapeDtypeStruct((t, D_ATTN), BF16),
            jax.ShapeDtypeStruct((t, D_ATTN), BF16),
            jax.ShapeDtypeStruct((t, D_ATTN), BF16),
            jax.ShapeDtypeStruct((t, ATTN_HEADS), F32),
            jax.ShapeDtypeStruct((t, D_CONV), BF16),
        ],
        scratch_shapes=[pltpu.VMEM((1, LANES), F32)],
        compiler_params=_params("arbitrary", "arbitrary"),
        name="inproj",
    )(x2d, g, w_packed, bf_pad, b_glu, tri)


def _attn_kernel(q_ref, k_ref, v_ref, c_ref, o_ref, ka_ref, va_ref):
    i = pl.program_id(1)
    seq = k_ref.shape[0]

    def halves(rows):
        lane = lax.broadcasted_iota(I32, (rows, LANES), 1)
        own = (lane < HEAD_DIM, lane >= HEAD_DIM)
        aug = (lane - HEAD_DIM, lane)
        return own, aug

    def split3_f32(c):
        return [p.astype(F32) for p in _split3(c)]

    @pl.when(i == 0)
    def _():
        own, aug = halves(seq)
        c = c_ref[...]
        for p in range(HEAD_PAIRS):
            k = k_ref[:, p * LANES:(p + 1) * LANES].astype(F32)
            v = v_ref[:, p * LANES:(p + 1) * LANES].astype(F32)
            for h in range(2):
                a = aug[h]
                hi, mid, lo = split3_f32(c[:, 2 * p + h:2 * p + h + 1])
                extra = jnp.where(a < 3, 1.0, jnp.where(a == 3, -hi, jnp.where(a == 4, -mid, jnp.where(a == 5, -lo, 0.0))))
                ka_ref[2 * p + h] = jnp.where(own[h], k, extra).astype(BF16)
                va_ref[2 * p + h] = jnp.where(own[h], v, 1.0).astype(BF16)

    row0 = pl.multiple_of(i * TQ, TQ)
    own, aug = halves(TQ)
    cq = c_ref[pl.ds(row0, TQ), :]
    qa = []
    for p in range(HEAD_PAIRS):
        q = q_ref[:, p * LANES:(p + 1) * LANES].astype(F32) * (HEAD_DIM ** -0.5)
        for h in range(2):
            a = aug[h]
            hi, mid, lo = split3_f32(cq[:, 2 * p + h:2 * p + h + 1])
            extra = jnp.where(a == 0, hi, jnp.where(a == 1, mid, jnp.where(a == 2, lo, jnp.where(a < 6, 1.0, 0.0))))
            qa.append(jnp.where(own[h], q, extra).astype(BF16))

    nt_dims = (((1,), (1,)), ((), ()))
    r = lax.broadcasted_iota(I32, (TQ, TQ), 0)
    cc = lax.broadcasted_iota(I32, (TQ, TQ), 1)

    def tile(n):
        past = n * TQ
        for p in range(HEAD_PAIRS):
            outs = []
            for hh in (2 * p, 2 * p + 1):
                sd = lax.dot_general(qa[hh], ka_ref[hh, past:past + TQ, :], nt_dims, preferred_element_type=F32)
                sd = jnp.where(r >= cc, sd, NEG_BIG)
                m = jnp.max(sd, axis=-1, keepdims=True)
                if n:
                    sp = lax.dot_general(qa[hh], ka_ref[hh, 0:past, :], nt_dims, preferred_element_type=F32)
                    m = jnp.maximum(m, jnp.max(sp, axis=-1, keepdims=True))
                acc = jnp.dot(jnp.exp(sd - m).astype(BF16), va_ref[hh, past:past + TQ, :],
                              preferred_element_type=F32)
                if n:
                    acc = acc + jnp.dot(jnp.exp(sp - m).astype(BF16), va_ref[hh, 0:past, :],
                                        preferred_element_type=F32)
                outs.append(acc / pltpu.roll(acc, HEAD_DIM, 1))
            o_ref[:, p * LANES:(p + 1) * LANES] = jnp.where(own[0], outs[0], outs[1]).astype(BF16)

    for n in range(seq // TQ):
        pl.when(i == n)(functools.partial(tile, n))


def _fox_attn(q, k, v, c, batch, seq):
    t = batch * seq
    nq = seq // TQ
    return pl.pallas_call(
        _attn_kernel,
        grid=(batch, nq),
        in_specs=[
            pl.BlockSpec((TQ, D_ATTN), lambda b, i: (b * nq + i, 0)),
            pl.BlockSpec((seq, D_ATTN), lambda b, i: (b, 0)),
            pl.BlockSpec((seq, D_ATTN), lambda b, i: (b, 0)),
            pl.BlockSpec((seq, ATTN_HEADS), lambda b, i: (b, 0)),
        ],
        out_specs=pl.BlockSpec((TQ, D_ATTN), lambda b, i: (b * nq + i, 0)),
        out_shape=jax.ShapeDtypeStruct((t, D_ATTN), BF16),
        scratch_shapes=[pltpu.VMEM((ATTN_HEADS, seq, LANES), BF16)] * 2,
        compiler_params=_params("arbitrary", "arbitrary"),
        name="fox_attn",
    )(q, k, v, c)


def _conv_kernel(z_ref, w_ref, b_ref, g_ref, beta_ref, o_ref, buf_ref):
    @pl.when(pl.program_id(1) == 0)
    def _():
        buf_ref[0:CONV_HALO, :] = jnp.zeros((CONV_HALO, D_CONV), F32)

    buf_ref[CONV_HALO:, :] = z_ref[...].astype(F32)
    first = CONV_HALO - (CONV_WIDTH - 1)
    for r in range(TS_CONV // CONV_ROWS):
        acc = jnp.zeros((CONV_ROWS, D_CONV), F32)
        for j in range(CONV_WIDTH):
            lo = r * CONV_ROWS + first + j
            acc = acc + buf_ref[lo:lo + CONV_ROWS, :] * w_ref[j:j + 1, :]
        y = acc + b_ref[...]
        mu = jnp.mean(y, axis=-1, keepdims=True)
        yc = y - mu
        y = yc * lax.rsqrt(jnp.mean(yc * yc, axis=-1, keepdims=True) + EPS) * g_ref[...] + beta_ref[...]
        o_ref[r * CONV_ROWS:(r + 1) * CONV_ROWS, :] = (y * jax.nn.sigmoid(y)).astype(BF16)
    buf_ref[0:CONV_HALO, :] = buf_ref[TS_CONV:TS_CONV + CONV_HALO, :]


def _conv(z, w_dw_pad, b_dw, ln_g, ln_b, batch, seq):
    t = batch * seq
    ns = seq // TS_CONV
    tok = lambda b, i: (b * ns + i, 0)
    const = lambda b, i: (0, 0)
    return pl.pallas_call(
        _conv_kernel,
        grid=(batch, ns),
        in_specs=[
            pl.BlockSpec((TS_CONV, D_CONV), tok),
            pl.BlockSpec((CONV_HALO, D_CONV), const),
            pl.BlockSpec((1, D_CONV), const),
            pl.BlockSpec((1, D_CONV), const),
            pl.BlockSpec((1, D_CONV), const),
        ],
        out_specs=pl.BlockSpec((TS_CONV, D_CONV), tok),
        out_shape=jax.ShapeDtypeStruct((t, D_CONV), BF16),
        scratch_shapes=[pltpu.VMEM((TS_CONV + CONV_HALO, D_CONV), F32)],
        compiler_params=_params("arbitrary", "arbitrary"),
        name="conv",
    )(z, w_dw_pad, b_dw, ln_g, ln_b)


def _memkv_kernel(m_ref, g_ref, w_ref, k_ref, v_ref):
    h = _rms(m_ref[...], g_ref[...]).astype(BF16)
    k = jnp.dot(h, w_ref[:, :D_MODEL], preferred_element_type=F32)
    k_ref[...] = (k * (MEM_HEAD_DIM ** -0.5)).astype(BF16)
    v_ref[...] = jnp.dot(h, w_ref[:, D_MODEL:], preferred_element_type=F32).astype(BF16)


def _mem_kv(mem2d, g, w_mkv, batch, n_mem):
    return pl.pallas_call(
        _memkv_kernel,
        grid=(batch,),
        in_specs=[
            pl.BlockSpec((n_mem, D_MODEL), lambda b: (b, 0)),
            pl.BlockSpec((1, D_MODEL), lambda b: (0, 0)),
            pl.BlockSpec((D_MODEL, 2 * D_MODEL), lambda b: (0, 0)),
        ],
        out_specs=[pl.BlockSpec((n_mem, D_MODEL), lambda b: (b, 0))] * 2,
        out_shape=[jax.ShapeDtypeStruct((batch * n_mem, D_MODEL), BF16)] * 2,
        compiler_params=_params("arbitrary"),
        name="mem_kv",
    )(mem2d, g, w_mkv)


def _mid_kernel(x_ref, a_ref, cv_ref, ag_ref, wout_ref, gm_ref, wmq_ref, km_ref, vm_ref, wmo_ref,
                gf_ref, wr_ref, br_ref, tri_ref,
                x2_ref, row_ref, rb_ref, cnt_ref):
    @pl.when(pl.program_id(0) == 0)
    def _():
        cnt_ref[...] = jnp.zeros_like(cnt_ref)

    a = _rms(a_ref[...].astype(F32), ag_ref[...]).astype(BF16)
    mix = jnp.dot(a, wout_ref[:D_ATTN, :], preferred_element_type=F32)
    mix = mix + jnp.dot(cv_ref[...], wout_ref[D_ATTN:, :], preferred_element_type=F32)
    x1 = x_ref[...] + mix

    h2 = _rms(x1, gm_ref[...]).astype(BF16)
    qm = jnp.dot(h2, wmq_ref[...], preferred_element_type=F32).astype(BF16)
    heads = []
    for hh in range(MEM_HEADS):
        sl = slice(hh * MEM_HEAD_DIM, (hh + 1) * MEM_HEAD_DIM)
        s = lax.dot_general(qm[:, sl], km_ref[:, sl], (((1,), (1,)), ((), ())), preferred_element_type=F32)
        p = jnp.exp(s - jnp.max(s, axis=-1, keepdims=True))
        l = jnp.sum(p, axis=-1, keepdims=True)
        o = jnp.dot(p.astype(BF16), vm_ref[:, sl], preferred_element_type=F32) / l
        heads.append(o.astype(BF16))
    o = jnp.concatenate(heads, axis=-1)
    x2 = x1 + jnp.dot(o, wmo_ref[...], preferred_element_type=F32)
    x2_ref[...] = x2

    h3 = _rms(x2, gf_ref[...])
    logits = jnp.dot(h3, wr_ref[...], preferred_element_type=F32, precision=lax.Precision.HIGHEST) + br_ref[...]
    lane = lax.broadcasted_iota(I32, logits.shape, 1)

    def top(vals):
        m = jnp.max(vals, axis=-1, keepdims=True)
        idx = jnp.min(jnp.where(vals == m, lane, LANES), axis=-1, keepdims=True)
        return m, idx

    gl = jnp.where(lane < N_GROUPS, logits, NEG_BIG)
    gmax, gidx = top(gl)
    g_top = 1.0 / jnp.sum(jnp.exp(gl - gmax), axis=-1, keepdims=True)
    first = ROUTE_OFF + EXPERTS_PER_GROUP * gidx
    el = jnp.where((lane >= first) & (lane < first + EXPERTS_PER_GROUP), logits, NEG_BIG)
    m1, i1 = top(el)
    m2, i2 = top(jnp.where(lane == i1, NEG_BIG, el))
    e2 = jnp.exp(m2 - m1)
    w1 = g_top / (1.0 + e2)
    w2 = g_top * e2 / (1.0 + e2)

    swap = i2 < i1
    a = jnp.minimum(i1, i2) - first
    b = jnp.maximum(i1, i2) - first
    pair = a * EXPERTS_PER_GROUP - ((a * (a + 1)) >> 1) + (b - a - 1)
    bucket = gidx * PAIRS_PER_GROUP + pair
    wa = jnp.where(swap, w2, w1)
    wb = jnp.where(swap, w1, w2)
    row_ref[:, :D_MODEL] = h3
    row_ref[:, D_MODEL:] = jnp.where(lane == 0, wa, jnp.where(lane == 1, wb, 0.0))

    hit = lane == bucket
    cum = cnt_ref[...] + jnp.dot(tri_ref[...], jnp.where(hit, 1.0, 0.0).astype(BF16), preferred_element_type=F32)
    rank = jnp.sum(jnp.where(hit, cum, 0.0), axis=-1, keepdims=True) - 1.0
    cnt_ref[...] = cum[-1:, :]
    l8 = lax.broadcasted_iota(I32, (TM_MID, SUBLANES), 1)
    rb_ref[...] = jnp.where(l8 == 0, bucket.astype(F32), jnp.where(l8 == 1, rank, 0.0))


def _mid(x2d, attn, conv, ag, w_out, gm, w_mq, kmem, vmem, w_mo, gf, w_route, b_route, batch, seq, n_mem):
    t = batch * seq
    per_b = seq // TM_MID
    tri = jnp.tril(jnp.ones((TM_MID, TM_MID), BF16))
    tok = lambda i: (i, 0)
    const = lambda i: (0, 0)
    memb = lambda i: (i // per_b, 0)
    return pl.pallas_call(
        _mid_kernel,
        grid=(t // TM_MID,),
        in_specs=[
            pl.BlockSpec((TM_MID, D_MODEL), tok),
            pl.BlockSpec((TM_MID, D_ATTN), tok),
            pl.BlockSpec((TM_MID, D_CONV), tok),
            pl.BlockSpec((1, D_ATTN), const),
            pl.BlockSpec((D_MODEL, D_MODEL), const),
            pl.BlockSpec((1, D_MODEL), const),
            pl.BlockSpec((D_MODEL, D_MODEL), const),
            pl.BlockSpec((n_mem, D_MODEL), memb),
            pl.BlockSpec((n_mem, D_MODEL), memb),
            pl.BlockSpec((D_MODEL, D_MODEL), const),
            pl.BlockSpec((1, D_MODEL), const),
            pl.BlockSpec((D_MODEL, LANES), const),
            pl.BlockSpec((1, LANES), const),
            pl.BlockSpec((TM_MID, TM_MID), const),
        ],
        out_specs=[
            pl.BlockSpec((TM_MID, D_MODEL), tok),
            pl.BlockSpec((TM_MID, D_ROW), tok),
            pl.BlockSpec((TM_MID, SUBLANES), tok),
            pl.BlockSpec((1, LANES), const),
        ],
        out_shape=[
            jax.ShapeDtypeStruct((t, D_MODEL), F32),
            jax.ShapeDtypeStruct((t, D_ROW), F32),
            jax.ShapeDtypeStruct((t, SUBLANES), F32),
            jax.ShapeDtypeStruct((1, LANES), F32),
        ],
        compiler_params=_params("arbitrary"),
        name="mid",
    )(x2d, attn, conv, ag, w_out, gm, w_mq, kmem, vmem, w_mo, gf, w_route, b_route, tri)


def _bucket_expert_tables():
    ea = np.zeros((1, LANES), np.float32)
    eb = np.zeros((1, LANES), np.float32)
    for g in range(N_GROUPS):
        n = 0
        for a in range(EXPERTS_PER_GROUP):
            for b in range(a + 1, EXPERTS_PER_GROUP):
                ea[0, g * PAIRS_PER_GROUP + n] = g * EXPERTS_PER_GROUP + a
                eb[0, g * PAIRS_PER_GROUP + n] = g * EXPERTS_PER_GROUP + b
                n += 1
    return jnp.asarray(ea), jnp.asarray(eb)


def _plan_kernel(cnt_ref, rb_ref, ea_ref, eb_ref, triu_ref, pos_ref, tinfo_ref):
    hi = lax.Precision.HIGHEST
    cnt = cnt_ref[...]
    padded = jnp.floor((cnt + (TMX - 1)) * (1.0 / TMX)) * TMX
    end = jnp.dot(padded, triu_ref[...], preferred_element_type=F32, precision=hi)
    base = end - padded

    @pl.when(pl.program_id(0) == 0)
    def _():
        n_tiles = tinfo_ref.shape[0]
        lane = lax.broadcasted_iota(I32, (n_tiles, LANES), 1)
        start = lax.broadcasted_iota(I32, (n_tiles, LANES), 0).astype(F32) * TMX
        done = (lane < N_BUCKETS) & (end <= start)
        tb = jnp.sum(jnp.where(done, 1, 0), axis=-1, keepdims=True)
        ea = jnp.sum(jnp.where(lane == tb, ea_ref[...], 0.0), axis=-1, keepdims=True)
        eb = jnp.sum(jnp.where(lane == tb, eb_ref[...], 0.0), axis=-1, keepdims=True)
        used = jnp.sum(jnp.where(lane == N_BUCKETS - 1, end, 0.0), axis=-1, keepdims=True) * (1.0 / TMX)
        l8 = lax.broadcasted_iota(I32, (n_tiles, SUBLANES), 1)
        tinfo_ref[...] = jnp.where(l8 == 0, ea, jnp.where(l8 == 1, eb, jnp.where(l8 == 2, used, 0.0))).astype(I32)

    rb = rb_ref[...]
    lane = lax.broadcasted_iota(I32, (rb.shape[0], LANES), 1)
    hit = lane == rb[:, 0:1].astype(I32)
    dest = jnp.where(hit, base + rb[:, 1:2], 0.0)
    pos = lax.dot_general(jnp.ones((SUBLANES, LANES), F32), dest, (((1,), (1,)), ((), ())),
                          preferred_element_type=F32, precision=hi)
    pos_ref[...] = pos.astype(I32)


def _moe_plan(counts, rb, n_tiles):
    t = rb.shape[0]
    ea, eb = _bucket_expert_tables()
    triu = jnp.triu(jnp.ones((LANES, LANES), F32))
    const = lambda i: (0, 0)
    return pl.pallas_call(
        _plan_kernel,
        grid=(t // TM_PLAN,),
        in_specs=[
            pl.BlockSpec((1, LANES), const),
            pl.BlockSpec((TM_PLAN, SUBLANES), lambda i: (i, 0)),
            pl.BlockSpec((1, LANES), const),
            pl.BlockSpec((1, LANES), const),
            pl.BlockSpec((LANES, LANES), const),
        ],
        out_specs=[
            pl.BlockSpec((SUBLANES, TM_PLAN), lambda i: (0, i)),
            pl.BlockSpec((n_tiles, SUBLANES), const),
        ],
        out_shape=[
            jax.ShapeDtypeStruct((SUBLANES, t), I32),
            jax.ShapeDtypeStruct((n_tiles, SUBLANES), I32),
        ],
        compiler_params=_params("arbitrary"),
        name="moe_plan",
    )(counts, rb, ea, eb, triu)


def _dispatch_kernel(pos_ref, rows_hbm, init_hbm, xs_hbm, sem):
    del init_hbm
    i = pl.program_id(0)
    t0 = i * TM_DISP

    def row_copy(t, p):
        return pltpu.make_async_copy(rows_hbm.at[pl.ds(t, 1), :], xs_hbm.at[pl.ds(p, 1), :], sem)

    def wait_tile():
        pltpu.make_async_copy(rows_hbm.at[pl.ds(0, TM_DISP), :], xs_hbm.at[pl.ds(0, TM_DISP), :], sem).wait()

    def issue(r, carry):
        row_copy(t0 + r, pos_ref[t0 + r]).start()
        return carry

    lax.fori_loop(0, TM_DISP, issue, 0, unroll=DMA_UNROLL)

    @pl.when(i > 0)
    def _():
        wait_tile()

    @pl.when(i == pl.num_programs(0) - 1)
    def _():
        wait_tile()


def _moe_dispatch(pos, rows, n_rows):
    t = rows.shape[0]
    init = jnp.zeros((n_rows, D_ROW), F32)
    return pl.pallas_call(
        _dispatch_kernel,
        grid_spec=pltpu.PrefetchScalarGridSpec(
            num_scalar_prefetch=1,
            grid=(t // TM_DISP,),
            in_specs=[pl.BlockSpec(memory_space=pl.ANY), pl.BlockSpec(memory_space=pl.ANY)],
            out_specs=pl.BlockSpec(memory_space=pl.ANY),
            scratch_shapes=[pltpu.SemaphoreType.DMA],
        ),
        out_shape=jax.ShapeDtypeStruct((n_rows, D_ROW), F32),
        input_output_aliases={2: 0},
        compiler_params=_params("arbitrary"),
        name="moe_dispatch",
    )(pos, rows, init)


def _experts_kernel(ea_ref, eb_ref, used_ref, xs_ref, wga_ref, wua_ref, wda_ref, wgb_ref, wub_ref, wdb_ref, ys_ref):
    del ea_ref, eb_ref

    @pl.when(pl.program_id(0) < used_ref[0])
    def _():
        x = xs_ref[:, :D_MODEL].astype(BF16)
        gates = xs_ref[:, D_MODEL:]
        y = None
        for col, wg_ref, wu_ref, wd_ref in ((0, wga_ref, wua_ref, wda_ref), (1, wgb_ref, wub_ref, wdb_ref)):
            hg = jnp.dot(x, wg_ref[...], preferred_element_type=F32)
            hu = jnp.dot(x, wu_ref[...], preferred_element_type=F32)
            act = (hg * jax.nn.sigmoid(hg) * hu * gates[:, col:col + 1]).astype(BF16)
            part = jnp.dot(act, wd_ref[...], preferred_element_type=F32)
            y = part if y is None else y + part
        ys_ref[...] = y

    @pl.when(pl.program_id(0) >= used_ref[0])
    def _():
        ys_ref[...] = jnp.zeros_like(ys_ref)


def _moe_experts(tile_ea, tile_eb, used, xs, w_gate, w_up, w_down):
    n_rows = xs.shape[0]
    n_tiles = n_rows // TMX
    row = lambda r, ea, eb, used: (jnp.minimum(r, used[0] - 1), 0)
    wa = lambda r, ea, eb, used: (ea[jnp.minimum(r, used[0] - 1)], 0, 0)
    wb = lambda r, ea, eb, used: (eb[jnp.minimum(r, used[0] - 1)], 0, 0)
    return pl.pallas_call(
        _experts_kernel,
        grid_spec=pltpu.PrefetchScalarGridSpec(
            num_scalar_prefetch=3,
            grid=(n_tiles,),
            in_specs=[
                pl.BlockSpec((TMX, D_ROW), row),
                pl.BlockSpec((None, D_MODEL, D_EXPERT), wa),
                pl.BlockSpec((None, D_MODEL, D_EXPERT), wa),
                pl.BlockSpec((None, D_EXPERT, D_MODEL), wa),
                pl.BlockSpec((None, D_MODEL, D_EXPERT), wb),
                pl.BlockSpec((None, D_MODEL, D_EXPERT), wb),
                pl.BlockSpec((None, D_EXPERT, D_MODEL), wb),
            ],
            out_specs=pl.BlockSpec((TMX, D_MODEL), lambda r, ea, eb, used: (r, 0)),
        ),
        out_shape=jax.ShapeDtypeStruct((n_rows, D_MODEL), F32),
        compiler_params=_params("arbitrary"),
        name="moe_experts",
    )(tile_ea, tile_eb, used, xs, w_gate, w_up, w_down, w_gate, w_up, w_down)


def _combine_kernel(pos_ref, x2_ref, fg_ref, ys_hbm, o_ref, ybuf, sem):
    i = pl.program_id(0)
    n = pl.num_programs(0)

    def gather(step, slot):
        t0 = step * TM_COMB

        def issue(r, carry):
            pltpu.make_async_copy(ys_hbm.at[pl.ds(pos_ref[t0 + r], 1), :], ybuf.at[slot, pl.ds(r, 1), :],
                                  sem.at[slot]).start()
            return carry

        lax.fori_loop(0, TM_COMB, issue, 0, unroll=DMA_UNROLL)

    @pl.when(i == 0)
    def _():
        gather(0, 0)

    slot = lax.rem(i, 2)

    @pl.when(i + 1 < n)
    def _():
        gather(i + 1, 1 - slot)

    pltpu.make_async_copy(ys_hbm.at[pl.ds(0, TM_COMB), :], ybuf.at[slot], sem.at[slot]).wait()
    o_ref[...] = _rms(x2_ref[...] + ybuf[slot], fg_ref[...])


def _moe_combine(pos, x2, final_g, ys):
    t = x2.shape[0]
    return pl.pallas_call(
        _combine_kernel,
        grid_spec=pltpu.PrefetchScalarGridSpec(
            num_scalar_prefetch=1,
            grid=(t // TM_COMB,),
            in_specs=[
                pl.BlockSpec((TM_COMB, D_MODEL), lambda i, pos: (i, 0)),
                pl.BlockSpec((1, D_MODEL), lambda i, pos: (0, 0)),
                pl.BlockSpec(memory_space=pl.ANY),
            ],
            out_specs=pl.BlockSpec((TM_COMB, D_MODEL), lambda i, pos: (i, 0)),
            scratch_shapes=[pltpu.VMEM((2, TM_COMB, D_MODEL), F32), pltpu.SemaphoreType.DMA((2,))],
        ),
        out_shape=jax.ShapeDtypeStruct((t, D_MODEL), F32),
        compiler_params=_params("arbitrary"),
        name="moe_combine",
    )(pos, x2, final_g, ys)


def kernel(x, mem, norm_mix_g, w_in, b_forget, b_glu, w_dw, b_dw, conv_ln_g, conv_ln_b, attn_out_g, w_out,
           norm_mem_g, mem_norm_g, w_mq, w_mkv, w_mo, norm_ffn_g, w_route_group, b_route_group,
           w_route_expert, b_route_expert, w_gate, w_up, w_down, final_g):
    batch, seq, _ = x.shape
    n_mem = mem.shape[1]
    assert norm_mix_g.shape[0] == 1, "single-layer block"
    t = batch * seq
    xs = x.reshape(t, D_MODEL)
    mem2d = mem.reshape(batch * n_mem, D_MODEL)
    row = lambda v: v.reshape(1, -1)

    wi = w_in[0]
    w_packed = jnp.concatenate(
        [wi[:, :3 * D_ATTN], jnp.pad(wi[:, 3 * D_ATTN:3 * D_ATTN + ATTN_HEADS], ((0, 0), (0, LANES - ATTN_HEADS))),
         wi[:, 3 * D_ATTN + ATTN_HEADS:]], axis=1).astype(BF16)
    bf_pad = jnp.pad(b_forget[0], (0, LANES - ATTN_HEADS)).reshape(1, LANES)
    q, k, v, c, z = _inproj(xs, row(norm_mix_g[0]), w_packed, bf_pad, row(b_glu[0]), batch, seq)
    attn = _fox_attn(q, k, v, c, batch, seq)
    w_dw_pad = jnp.pad(w_dw[0], ((0, CONV_HALO - CONV_WIDTH), (0, 0)))
    conv = _conv(z, w_dw_pad, row(b_dw[0]), row(conv_ln_g[0]), row(conv_ln_b[0]), batch, seq)
    kmem, vmem = _mem_kv(mem2d, row(mem_norm_g[0]), w_mkv[0].astype(BF16), batch, n_mem)

    w_route = jnp.concatenate(
        [w_route_group[0], w_route_expert[0].transpose(1, 0, 2).reshape(D_MODEL, N_EXPERTS)], axis=1)
    w_route = jnp.pad(w_route, ((0, 0), (0, LANES - N_GROUPS - N_EXPERTS)))
    b_route = jnp.pad(jnp.concatenate([b_route_group[0], b_route_expert[0].reshape(-1)]),
                      (0, LANES - N_GROUPS - N_EXPERTS)).reshape(1, LANES)
    x2, rows, rb, counts = _mid(xs, attn, conv, row(attn_out_g[0]), w_out[0].astype(BF16), row(norm_mem_g[0]),
                                w_mq[0].astype(BF16), kmem, vmem, w_mo[0].astype(BF16), row(norm_ffn_g[0]),
                                w_route, b_route, batch, seq, n_mem)

    n_tiles = t // TMX + N_BUCKETS
    pos8, tinfo = _moe_plan(counts, rb, n_tiles)
    pos = pos8[0]
    sorted_rows = _moe_dispatch(pos, rows, n_tiles * TMX)
    ys = _moe_experts(tinfo[:, 0], tinfo[:, 1], tinfo[:1, 2], sorted_rows,
                      w_gate[0].astype(BF16), w_up[0].astype(BF16), w_down[0].astype(BF16))
    out = _moe_combine(pos, x2, row(final_g), ys)
    return out.reshape(batch, seq, D_MODEL)
```

```python
import functools

import jax
import jax.numpy as jnp
import numpy as np
from jax import lax
from jax.experimental import pallas as pl
from jax.experimental.pallas import tpu as pltpu

F32 = jnp.float32
BF16 = jnp.bfloat16
I32 = jnp.int32

D_MODEL = 1024
ATTN_HEADS = 8
HEAD_DIM = 64
D_ATTN = ATTN_HEADS * HEAD_DIM
D_CONV = D_MODEL - D_ATTN
CONV_WIDTH = 31
MEM_HEADS = 4
MEM_HEAD_DIM = D_MODEL // MEM_HEADS
N_GROUPS = 4
EXPERTS_PER_GROUP = 8
N_EXPERTS = N_GROUPS * EXPERTS_PER_GROUP
D_EXPERT = D_MODEL // 2
EPS = 1e-6

LANES = 128
SUBLANES = 8
HEAD_PAIRS = ATTN_HEADS // 2
COL_Q = 0
COL_K = COL_Q + D_ATTN
COL_V = COL_K + D_ATTN
COL_F = COL_V + D_ATTN
COL_C = COL_F + LANES
D_IN_PACKED = COL_C + 2 * D_CONV
ROUTE_OFF = N_GROUPS
PAIRS_PER_GROUP = EXPERTS_PER_GROUP * (EXPERTS_PER_GROUP - 1) // 2
N_BUCKETS = N_GROUPS * PAIRS_PER_GROUP
D_ROW = D_MODEL + LANES
NEG_BIG = -1e30
VMEM_LIMIT = 48 * 1024 * 1024

TM_IN = 512
TQ = 256
TS_CONV = 256
CONV_HALO = 32
CONV_ROWS = 32
TM_MID = 512
TM_PLAN = 2048
TM_DISP = 1024
TMX = 256
TM_COMB = 512
DMA_UNROLL = 8

assert N_BUCKETS <= LANES


def _rms(x, g):
    return x * lax.rsqrt(jnp.mean(x * x, axis=-1, keepdims=True) + EPS) * g


def _split3(c):
    hi = c.astype(BF16)
    r = c - hi.astype(F32)
    mid = r.astype(BF16)
    lo = (r - mid.astype(F32)).astype(BF16)
    return hi, mid, lo


def _params(*sem):
    return pltpu.CompilerParams(dimension_semantics=sem, vmem_limit_bytes=VMEM_LIMIT)


def _inproj_kernel(x_ref, g_ref, w_ref, bf_ref, bglu_ref, tri_ref,
                   q_ref, k_ref, v_ref, c_ref, z_ref, carry_ref):
    @pl.when(pl.program_id(1) == 0)
    def _():
        carry_ref[...] = jnp.zeros_like(carry_ref)

    h = _rms(x_ref[...], g_ref[...]).astype(BF16)
    q_ref[...] = jnp.dot(h, w_ref[:, COL_Q:COL_K], preferred_element_type=F32).astype(BF16)
    k_ref[...] = jnp.dot(h, w_ref[:, COL_K:COL_V], preferred_element_type=F32).astype(BF16)
    v_ref[...] = jnp.dot(h, w_ref[:, COL_V:COL_F], preferred_element_type=F32).astype(BF16)

    fl = jnp.dot(h, w_ref[:, COL_F:COL_C], preferred_element_type=F32) + bf_ref[...]
    lf = jnp.minimum(fl, 0.0) - jnp.log(1.0 + jnp.exp(-jnp.abs(fl)))
    tri = tri_ref[...]
    c = carry_ref[...]
    for part in _split3(lf):
        c = c + jnp.dot(tri, part, preferred_element_type=F32)
    carry_ref[...] = c[-1:, :]
    c_ref[...] = c[:, :ATTN_HEADS]

    u = jnp.dot(h, w_ref[:, COL_C:D_IN_PACKED], preferred_element_type=F32) + bglu_ref[...]
    z_ref[...] = (u[:, :D_CONV] * jax.nn.sigmoid(u[:, D_CONV:])).astype(BF16)


def _inproj(x2d, g, w_packed, bf_pad, b_glu, batch, seq):
    t = batch * seq
    nt = seq // TM_IN
    tri = jnp.tril(jnp.ones((TM_IN, TM_IN), BF16))
    tok = lambda b, i: (b * nt + i, 0)
    const = lambda b, i: (0, 0)
    return pl.pallas_call(
        _inproj_kernel,
        grid=(batch, nt),
        in_specs=[
            pl.BlockSpec((TM_IN, D_MODEL), tok),
            pl.BlockSpec((1, D_MODEL), const),
            pl.BlockSpec((D_MODEL, D_IN_PACKED), const),
            pl.BlockSpec((1, LANES), const),
            pl.BlockSpec((1, 2 * D_CONV), const),
            pl.BlockSpec((TM_IN, TM_IN), const),
        ],
        out_specs=[
            pl.BlockSpec((TM_IN, D_ATTN), tok),
            pl.BlockSpec((TM_IN, D_ATTN), tok),
            pl.BlockSpec((TM_IN, D_ATTN), tok),
            pl.BlockSpec((TM_IN, ATTN_HEADS), tok),
            pl.BlockSpec((TM_IN, D_CONV), tok),
        ],
        out_shape=[
            jax.ShapeDtypeStruct((t, D_ATTN), BF16),
            jax.ShapeDtypeStruct((t, D_ATTN), BF16),
            jax.ShapeDtypeStruct((t, D_ATTN), BF16),
            jax.ShapeDtypeStruct((t, ATTN_HEADS), F32),
            jax.ShapeDtypeStruct((t, D_CONV), BF16),
        ],
        scratch_shapes=[pltpu.VMEM((1, LANES), F32)],
        compiler_params=_params("arbitrary", "arbitrary"),
        name="inproj",
    )(x2d, g, w_packed, bf_pad, b_glu, tri)


def _attn_kernel(q_ref, k_ref, v_ref, c_ref, o_ref, ka_ref, va_ref, qa_ref, m_ref, acc_ref):
    i = pl.program_id(1)
    seq = k_ref.shape[0]

    def halves(rows):
        lane = lax.broadcasted_iota(I32, (rows, LANES), 1)
        own = (lane < HEAD_DIM, lane >= HEAD_DIM)
        aug = (lane - HEAD_DIM, lane)
        return own, aug

    def split3_f32(c):
        return [p.astype(F32) for p in _split3(c)]

    @pl.when(i == 0)
    def _():
        own, aug = halves(seq)
        c = c_ref[...]
        for p in range(HEAD_PAIRS):
            k = k_ref[:, p * LANES:(p + 1) * LANES].astype(F32)
            v = v_ref[:, p * LANES:(p + 1) * LANES].astype(F32)
            for h in range(2):
                a = aug[h]
                hi, mid, lo = split3_f32(c[:, 2 * p + h:2 * p + h + 1])
                extra = jnp.where(a < 3, 1.0, jnp.where(a == 3, -hi, jnp.where(a == 4, -mid, jnp.where(a == 5, -lo, 0.0))))
                ka_ref[2 * p + h] = jnp.where(own[h], k, extra).astype(BF16)
                va_ref[2 * p + h] = jnp.where(own[h], v, 1.0).astype(BF16)

    row0 = pl.multiple_of(i * TQ, TQ)
    own, aug = halves(TQ)
    cq = c_ref[pl.ds(row0, TQ), :]
    for p in range(HEAD_PAIRS):
        q = q_ref[:, p * LANES:(p + 1) * LANES].astype(F32) * (HEAD_DIM ** -0.5)
        for h in range(2):
            a = aug[h]
            hi, mid, lo = split3_f32(cq[:, 2 * p + h:2 * p + h + 1])
            extra = jnp.where(a == 0, hi, jnp.where(a == 1, mid, jnp.where(a == 2, lo, jnp.where(a < 6, 1.0, 0.0))))
            qa_ref[2 * p + h] = jnp.where(own[h], q, extra).astype(BF16)
    m_ref[...] = jnp.full(m_ref.shape, NEG_BIG, F32)
    acc_ref[...] = jnp.zeros_like(acc_ref)

    nt_dims = (((1,), (1,)), ((), ()))

    def chunk(j, masked):
        k0 = pl.multiple_of(j * TQ, TQ)
        for hh in range(ATTN_HEADS):
            s = lax.dot_general(qa_ref[hh], ka_ref[hh, pl.ds(k0, TQ), :], nt_dims, preferred_element_type=F32)
            if masked:
                r = lax.broadcasted_iota(I32, (TQ, TQ), 0)
                cc = lax.broadcasted_iota(I32, (TQ, TQ), 1)
                s = jnp.where(r >= cc, s, NEG_BIG)
            m_old = m_ref[hh]
            m_new = jnp.maximum(m_old, jnp.max(s, axis=-1, keepdims=True))
            p = jnp.concatenate([jnp.exp(s[:, n * LANES:(n + 1) * LANES] - m_new) for n in range(TQ // LANES)],
                                axis=1).astype(BF16)
            pv = jnp.dot(p, va_ref[hh, pl.ds(k0, TQ), :], preferred_element_type=F32)
            acc_ref[hh] = jnp.exp(m_old - m_new) * acc_ref[hh] + pv
            m_ref[hh] = m_new

    def body(j, carry):
        chunk(j, False)
        return carry

    lax.fori_loop(0, i, body, 0)
    chunk(i, True)
    for p in range(HEAD_PAIRS):
        outs = [acc_ref[hh] / pltpu.roll(acc_ref[hh], HEAD_DIM, 1) for hh in (2 * p, 2 * p + 1)]
        o_ref[:, p * LANES:(p + 1) * LANES] = jnp.where(own[0], outs[0], outs[1]).astype(BF16)


def _fox_attn(q, k, v, c, batch, seq):
    t = batch * seq
    nq = seq // TQ
    return pl.pallas_call(
        _attn_kernel,
        grid=(batch, nq),
        in_specs=[
            pl.BlockSpec((TQ, D_ATTN), lambda b, i: (b * nq + i, 0)),
            pl.BlockSpec((seq, D_ATTN), lambda b, i: (b, 0)),
            pl.BlockSpec((seq, D_ATTN), lambda b, i: (b, 0)),
            pl.BlockSpec((seq, ATTN_HEADS), lambda b, i: (b, 0)),
        ],
        out_specs=pl.BlockSpec((TQ, D_ATTN), lambda b, i: (b * nq + i, 0)),
        out_shape=jax.ShapeDtypeStruct((t, D_ATTN), BF16),
        scratch_shapes=[
            pltpu.VMEM((ATTN_HEADS, seq, LANES), BF16),
            pltpu.VMEM((ATTN_HEADS, seq, LANES), BF16),
            pltpu.VMEM((ATTN_HEADS, TQ, LANES), BF16),
            pltpu.VMEM((ATTN_HEADS, TQ, LANES), F32),
            pltpu.VMEM((ATTN_HEADS, TQ, LANES), F32),
        ],
        compiler_params=_params("arbitrary", "arbitrary"),
        name="fox_attn",
    )(q, k, v, c)


def _conv_kernel(z_ref, w_ref, b_ref, g_ref, beta_ref, o_ref, buf_ref, sh_ref):
    @pl.when(pl.program_id(1) == 0)
    def _():
        buf_ref[0:CONV_HALO, :] = jnp.zeros((CONV_HALO, D_CONV), F32)

    buf_ref[CONV_HALO:, :] = z_ref[...].astype(F32)
    span = TS_CONV + CONV_HALO - SUBLANES
    for s in range(1, SUBLANES):
        sh_ref[s, 0:span, :] = buf_ref[s:s + span, :]
    first = CONV_HALO - (CONV_WIDTH - 1)
    for r in range(TS_CONV // CONV_ROWS):
        acc = jnp.zeros((CONV_ROWS, D_CONV), F32)
        for j in range(CONV_WIDTH):
            s = (first + j) % SUBLANES
            lo = r * CONV_ROWS + first + j - s
            tap = sh_ref[s, lo:lo + CONV_ROWS, :] if s else buf_ref[lo:lo + CONV_ROWS, :]
            acc = acc + tap * w_ref[j:j + 1, :]
        y = acc + b_ref[...]
        mu = jnp.mean(y, axis=-1, keepdims=True)
        yc = y - mu
        y = yc * lax.rsqrt(jnp.mean(yc * yc, axis=-1, keepdims=True) + EPS) * g_ref[...] + beta_ref[...]
        o_ref[r * CONV_ROWS:(r + 1) * CONV_ROWS, :] = (y * jax.nn.sigmoid(y)).astype(BF16)
    buf_ref[0:CONV_HALO, :] = buf_ref[TS_CONV:TS_CONV + CONV_HALO, :]


def _conv(z, w_dw_pad, b_dw, ln_g, ln_b, batch, seq):
    t = batch * seq
    ns = seq // TS_CONV
    tok = lambda b, i: (b * ns + i, 0)
    const = lambda b, i: (0, 0)
    return pl.pallas_call(
        _conv_kernel,
        grid=(batch, ns),
        in_specs=[
            pl.BlockSpec((TS_CONV, D_CONV), tok),
            pl.BlockSpec((CONV_HALO, D_CONV), const),
            pl.BlockSpec((1, D_CONV), const),
            pl.BlockSpec((1, D_CONV), const),
            pl.BlockSpec((1, D_CONV), const),
        ],
        out_specs=pl.BlockSpec((TS_CONV, D_CONV), tok),
        out_shape=jax.ShapeDtypeStruct((t, D_CONV), BF16),
        scratch_shapes=[pltpu.VMEM((TS_CONV + CONV_HALO, D_CONV), F32),
                        pltpu.VMEM((SUBLANES, TS_CONV + CONV_HALO, D_CONV), F32)],
        compiler_params=_params("arbitrary", "arbitrary"),
        name="conv",
    )(z, w_dw_pad, b_dw, ln_g, ln_b)


def _memkv_kernel(m_ref, g_ref, w_ref, k_ref, v_ref):
    h = _rms(m_ref[...], g_ref[...]).astype(BF16)
    k = jnp.dot(h, w_ref[:, :D_MODEL], preferred_element_type=F32)
    k_ref[...] = (k * (MEM_HEAD_DIM ** -0.5)).astype(BF16)
    v_ref[...] = jnp.dot(h, w_ref[:, D_MODEL:], preferred_element_type=F32).astype(BF16)


def _mem_kv(mem2d, g, w_mkv, batch, n_mem):
    return pl.pallas_call(
        _memkv_kernel,
        grid=(batch,),
        in_specs=[
            pl.BlockSpec((n_mem, D_MODEL), lambda b: (b, 0)),
            pl.BlockSpec((1, D_MODEL), lambda b: (0, 0)),
            pl.BlockSpec((D_MODEL, 2 * D_MODEL), lambda b: (0, 0)),
        ],
        out_specs=[pl.BlockSpec((n_mem, D_MODEL), lambda b: (b, 0))] * 2,
        out_shape=[jax.ShapeDtypeStruct((batch * n_mem, D_MODEL), BF16)] * 2,
        compiler_params=_params("arbitrary"),
        name="mem_kv",
    )(mem2d, g, w_mkv)


def _mid_kernel(x_ref, a_ref, cv_ref, ag_ref, wout_ref, gm_ref, wmq_ref, km_ref, vm_ref, wmo_ref,
                gf_ref, wr_ref, br_ref, tri_ref,
                x2_ref, row_ref, rb_ref, cnt_ref):
    @pl.when(pl.program_id(0) == 0)
    def _():
        cnt_ref[...] = jnp.zeros_like(cnt_ref)

    a = _rms(a_ref[...].astype(F32), ag_ref[...]).astype(BF16)
    mix = jnp.dot(a, wout_ref[:D_ATTN, :], preferred_element_type=F32)
    mix = mix + jnp.dot(cv_ref[...], wout_ref[D_ATTN:, :], preferred_element_type=F32)
    x1 = x_ref[...] + mix

    h2 = _rms(x1, gm_ref[...]).astype(BF16)
    qm = jnp.dot(h2, wmq_ref[...], preferred_element_type=F32).astype(BF16)
    heads = []
    for hh in range(MEM_HEADS):
        sl = slice(hh * MEM_HEAD_DIM, (hh + 1) * MEM_HEAD_DIM)
        s = lax.dot_general(qm[:, sl], km_ref[:, sl], (((1,), (1,)), ((), ())), preferred_element_type=F32)
        p = jnp.exp(s - jnp.max(s, axis=-1, keepdims=True))
        l = jnp.sum(p, axis=-1, keepdims=True)
        o = jnp.dot(p.astype(BF16), vm_ref[:, sl], preferred_element_type=F32) / l
        heads.append(o.astype(BF16))
    o = jnp.concatenate(heads, axis=-1)
    x2 = x1 + jnp.dot(o, wmo_ref[...], preferred_element_type=F32)
    x2_ref[...] = x2

    h3 = _rms(x2, gf_ref[...])
    h_hi = h3.astype(BF16)
    h_lo = (h3 - h_hi.astype(F32)).astype(BF16)
    logits = (jnp.dot(h_hi, wr_ref[0], preferred_element_type=F32)
              + jnp.dot(h_hi, wr_ref[1], preferred_element_type=F32)
              + jnp.dot(h_lo, wr_ref[0], preferred_element_type=F32)) + br_ref[...]
    lane = lax.broadcasted_iota(I32, logits.shape, 1)

    def top(vals):
        m = jnp.max(vals, axis=-1, keepdims=True)
        idx = jnp.min(jnp.where(vals == m, lane, LANES), axis=-1, keepdims=True)
        return m, idx

    gl = jnp.where(lane < N_GROUPS, logits, NEG_BIG)
    gmax, gidx = top(gl)
    g_top = 1.0 / jnp.sum(jnp.exp(gl - gmax), axis=-1, keepdims=True)
    first = ROUTE_OFF + EXPERTS_PER_GROUP * gidx
    el = jnp.where((lane >= first) & (lane < first + EXPERTS_PER_GROUP), logits, NEG_BIG)
    m1, i1 = top(el)
    m2, i2 = top(jnp.where(lane == i1, NEG_BIG, el))
    e2 = jnp.exp(m2 - m1)
    w1 = g_top / (1.0 + e2)
    w2 = g_top * e2 / (1.0 + e2)

    swap = i2 < i1
    a = jnp.minimum(i1, i2) - first
    b = jnp.maximum(i1, i2) - first
    pair = a * EXPERTS_PER_GROUP - ((a * (a + 1)) >> 1) + (b - a - 1)
    bucket = gidx * PAIRS_PER_GROUP + pair
    wa = jnp.where(swap, w2, w1)
    wb = jnp.where(swap, w1, w2)
    row_ref[:, :D_MODEL] = h3
    row_ref[:, D_MODEL:] = jnp.where(lane == 0, wa, jnp.where(lane == 1, wb, 0.0))

    hit = lane == bucket
    cum = cnt_ref[...] + jnp.dot(tri_ref[...], jnp.where(hit, 1.0, 0.0).astype(BF16), preferred_element_type=F32)
    rank = jnp.sum(jnp.where(hit, cum, 0.0), axis=-1, keepdims=True) - 1.0
    cnt_ref[...] = cum[-1:, :]
    l8 = lax.broadcasted_iota(I32, (TM_MID, SUBLANES), 1)
    rb_ref[...] = jnp.where(l8 == 0, bucket.astype(F32), jnp.where(l8 == 1, rank, 0.0))


def _mid(x2d, attn, conv, ag, w_out, gm, w_mq, kmem, vmem, w_mo, gf, w_route, b_route, batch, seq, n_mem):
    t = batch * seq
    per_b = seq // TM_MID
    tri = jnp.tril(jnp.ones((TM_MID, TM_MID), BF16))
    tok = lambda i: (i, 0)
    const = lambda i: (0, 0)
    memb = lambda i: (i // per_b, 0)
    return pl.pallas_call(
        _mid_kernel,
        grid=(t // TM_MID,),
        in_specs=[
            pl.BlockSpec((TM_MID, D_MODEL), tok),
            pl.BlockSpec((TM_MID, D_ATTN), tok),
            pl.BlockSpec((TM_MID, D_CONV), tok),
            pl.BlockSpec((1, D_ATTN), const),
            pl.BlockSpec((D_MODEL, D_MODEL), const),
            pl.BlockSpec((1, D_MODEL), const),
            pl.BlockSpec((D_MODEL, D_MODEL), const),
            pl.BlockSpec((n_mem, D_MODEL), memb),
            pl.BlockSpec((n_mem, D_MODEL), memb),
            pl.BlockSpec((D_MODEL, D_MODEL), const),
            pl.BlockSpec((1, D_MODEL), const),
            pl.BlockSpec((2, D_MODEL, LANES), lambda i: (0, 0, 0)),
            pl.BlockSpec((1, LANES), const),
            pl.BlockSpec((TM_MID, TM_MID), const),
        ],
        out_specs=[
            pl.BlockSpec((TM_MID, D_MODEL), tok),
            pl.BlockSpec((TM_MID, D_ROW), tok),
            pl.BlockSpec((TM_MID, SUBLANES), tok),
            pl.BlockSpec((1, LANES), const),
        ],
        out_shape=[
            jax.ShapeDtypeStruct((t, D_MODEL), F32),
            jax.ShapeDtypeStruct((t, D_ROW), F32),
            jax.ShapeDtypeStruct((t, SUBLANES), F32),
            jax.ShapeDtypeStruct((1, LANES), F32),
        ],
        compiler_params=_params("arbitrary"),
        name="mid",
    )(x2d, attn, conv, ag, w_out, gm, w_mq, kmem, vmem, w_mo, gf, w_route, b_route, tri)


def _bucket_expert_tables():
    ea = np.zeros((1, LANES), np.float32)
    eb = np.zeros((1, LANES), np.float32)
    for g in range(N_GROUPS):
        n = 0
        for a in range(EXPERTS_PER_GROUP):
            for b in range(a + 1, EXPERTS_PER_GROUP):
                ea[0, g * PAIRS_PER_GROUP + n] = g * EXPERTS_PER_GROUP + a
                eb[0, g * PAIRS_PER_GROUP + n] = g * EXPERTS_PER_GROUP + b
                n += 1
    return jnp.asarray(ea), jnp.asarray(eb)


def _plan_kernel(cnt_ref, rb_ref, ea_ref, eb_ref, triu_ref, pos_ref, tinfo_ref):
    hi = lax.Precision.HIGHEST
    cnt = cnt_ref[...]
    padded = jnp.floor((cnt + (TMX - 1)) * (1.0 / TMX)) * TMX
    end = jnp.dot(padded, triu_ref[...], preferred_element_type=F32, precision=hi)
    base = end - padded

    @pl.when(pl.program_id(0) == 0)
    def _():
        n_tiles = tinfo_ref.shape[0]
        lane = lax.broadcasted_iota(I32, (n_tiles, LANES), 1)
        start = lax.broadcasted_iota(I32, (n_tiles, LANES), 0).astype(F32) * TMX
        done = (lane < N_BUCKETS) & (end <= start)
        tb = jnp.sum(jnp.where(done, 1, 0), axis=-1, keepdims=True)
        ea = jnp.sum(jnp.where(lane == tb, ea_ref[...], 0.0), axis=-1, keepdims=True)
        eb = jnp.sum(jnp.where(lane == tb, eb_ref[...], 0.0), axis=-1, keepdims=True)
        used = jnp.sum(jnp.where(lane == N_BUCKETS - 1, end, 0.0), axis=-1, keepdims=True) * (1.0 / TMX)
        l8 = lax.broadcasted_iota(I32, (n_tiles, SUBLANES), 1)
        tinfo_ref[...] = jnp.where(l8 == 0, ea, jnp.where(l8 == 1, eb, jnp.where(l8 == 2, used, 0.0))).astype(I32)

    rb = rb_ref[...]
    lane = lax.broadcasted_iota(I32, (rb.shape[0], LANES), 1)
    hit = lane == rb[:, 0:1].astype(I32)
    dest = jnp.where(hit, base + rb[:, 1:2], 0.0)
    pos = lax.dot_general(jnp.ones((SUBLANES, LANES), F32), dest, (((1,), (1,)), ((), ())),
                          preferred_element_type=F32, precision=hi)
    pos_ref[...] = pos.astype(I32)


def _moe_plan(counts, rb, n_tiles):
    t = rb.shape[0]
    ea, eb = _bucket_expert_tables()
    triu = jnp.triu(jnp.ones((LANES, LANES), F32))
    const = lambda i: (0, 0)
    return pl.pallas_call(
        _plan_kernel,
        grid=(t // TM_PLAN,),
        in_specs=[
            pl.BlockSpec((1, LANES), const),
            pl.BlockSpec((TM_PLAN, SUBLANES), lambda i: (i, 0)),
            pl.BlockSpec((1, LANES), const),
            pl.BlockSpec((1, LANES), const),
            pl.BlockSpec((LANES, LANES), const),
        ],
        out_specs=[
            pl.BlockSpec((SUBLANES, TM_PLAN), lambda i: (0, i)),
            pl.BlockSpec((n_tiles, SUBLANES), const),
        ],
        out_shape=[
            jax.ShapeDtypeStruct((SUBLANES, t), I32),
            jax.ShapeDtypeStruct((n_tiles, SUBLANES), I32),
        ],
        compiler_params=_params("arbitrary"),
        name="moe_plan",
    )(counts, rb, ea, eb, triu)


def _dispatch_kernel(pos_ref, rows_ref, init_hbm, xs_hbm, sem):
    del init_hbm
    t0 = pl.program_id(0) * TM_DISP

    def issue(r, carry):
        pltpu.make_async_copy(rows_ref.at[pl.ds(r, 1), :], xs_hbm.at[pl.ds(pos_ref[t0 + r], 1), :], sem).start()
        return carry

    lax.fori_loop(0, TM_DISP, issue, 0, unroll=DMA_UNROLL)
    pltpu.make_async_copy(rows_ref, xs_hbm.at[pl.ds(0, TM_DISP), :], sem).wait()


def _moe_dispatch(pos, rows, n_rows):
    t = rows.shape[0]
    init = jnp.zeros((n_rows, D_ROW), F32)
    return pl.pallas_call(
        _dispatch_kernel,
        grid_spec=pltpu.PrefetchScalarGridSpec(
            num_scalar_prefetch=1,
            grid=(t // TM_DISP,),
            in_specs=[pl.BlockSpec((TM_DISP, D_ROW), lambda i, pos: (i, 0)), pl.BlockSpec(memory_space=pl.ANY)],
            out_specs=pl.BlockSpec(memory_space=pl.ANY),
            scratch_shapes=[pltpu.SemaphoreType.DMA],
        ),
        out_shape=jax.ShapeDtypeStruct((n_rows, D_ROW), F32),
        input_output_aliases={2: 0},
        compiler_params=_params("arbitrary"),
        name="moe_dispatch",
    )(pos, rows, init)


def _experts_kernel(ea_ref, eb_ref, used_ref, xs_ref, wga_ref, wua_ref, wda_ref, wgb_ref, wub_ref, wdb_ref, ys_ref):
    del ea_ref, eb_ref

    @pl.when(pl.program_id(0) < used_ref[0])
    def _():
        x = xs_ref[:, :D_MODEL].astype(BF16)
        gates = xs_ref[:, D_MODEL:]
        y = None
        for col, wg_ref, wu_ref, wd_ref in ((0, wga_ref, wua_ref, wda_ref), (1, wgb_ref, wub_ref, wdb_ref)):
            hg = jnp.dot(x, wg_ref[...], preferred_element_type=F32)
            hu = jnp.dot(x, wu_ref[...], preferred_element_type=F32)
            act = (hg * jax.nn.sigmoid(hg) * hu * gates[:, col:col + 1]).astype(BF16)
            part = jnp.dot(act, wd_ref[...], preferred_element_type=F32)
            y = part if y is None else y + part
        ys_ref[...] = y

    @pl.when(pl.program_id(0) >= used_ref[0])
    def _():
        ys_ref[...] = jnp.zeros_like(ys_ref)


def _moe_experts(tile_ea, tile_eb, used, xs, w_gate, w_up, w_down):
    n_rows = xs.shape[0]
    n_tiles = n_rows // TMX
    row = lambda r, ea, eb, used: (jnp.minimum(r, used[0] - 1), 0)
    wa = lambda r, ea, eb, used: (ea[jnp.minimum(r, used[0] - 1)], 0, 0)
    wb = lambda r, ea, eb, used: (eb[jnp.minimum(r, used[0] - 1)], 0, 0)
    return pl.pallas_call(
        _experts_kernel,
        grid_spec=pltpu.PrefetchScalarGridSpec(
            num_scalar_prefetch=3,
            grid=(n_tiles,),
            in_specs=[
                pl.BlockSpec((TMX, D_ROW), row),
                pl.BlockSpec((None, D_MODEL, D_EXPERT), wa),
                pl.BlockSpec((None, D_MODEL, D_EXPERT), wa),
                pl.BlockSpec((None, D_EXPERT, D_MODEL), wa),
                pl.BlockSpec((None, D_MODEL, D_EXPERT), wb),
                pl.BlockSpec((None, D_MODEL, D_EXPERT), wb),
                pl.BlockSpec((None, D_EXPERT, D_MODEL), wb),
            ],
            out_specs=pl.BlockSpec((TMX, D_MODEL), lambda r, ea, eb, used: (r, 0)),
        ),
        out_shape=jax.ShapeDtypeStruct((n_rows, D_MODEL), F32),
        compiler_params=_params("arbitrary"),
        name="moe_experts",
    )(tile_ea, tile_eb, used, xs, w_gate, w_up, w_down, w_gate, w_up, w_down)


def _combine_kernel(pos_ref, x2_ref, fg_ref, ys_hbm, o_ref, ybuf, sem):
    i = pl.program_id(0)
    n = pl.num_programs(0)

    def gather(step, slot):
        t0 = step * TM_COMB

        def issue(r, carry):
            pltpu.make_async_copy(ys_hbm.at[pl.ds(pos_ref[t0 + r], 1), :], ybuf.at[slot, pl.ds(r, 1), :],
                                  sem.at[slot]).start()
            return carry

        lax.fori_loop(0, TM_COMB, issue, 0, unroll=DMA_UNROLL)

    @pl.when(i == 0)
    def _():
        gather(0, 0)

    slot = lax.rem(i, 2)

    @pl.when(i + 1 < n)
    def _():
        gather(i + 1, 1 - slot)

    pltpu.make_async_copy(ys_hbm.at[pl.ds(0, TM_COMB), :], ybuf.at[slot], sem.at[slot]).wait()
    o_ref[...] = _rms(x2_ref[...] + ybuf[slot], fg_ref[...])


def _moe_combine(pos, x2, final_g, ys):
    t = x2.shape[0]
    return pl.pallas_call(
        _combine_kernel,
        grid_spec=pltpu.PrefetchScalarGridSpec(
            num_scalar_prefetch=1,
            grid=(t // TM_COMB,),
            in_specs=[
                pl.BlockSpec((TM_COMB, D_MODEL), lambda i, pos: (i, 0)),
                pl.BlockSpec((1, D_MODEL), lambda i, pos: (0, 0)),
                pl.BlockSpec(memory_space=pl.ANY),
            ],
            out_specs=pl.BlockSpec((TM_COMB, D_MODEL), lambda i, pos: (i, 0)),
            scratch_shapes=[pltpu.VMEM((2, TM_COMB, D_MODEL), F32), pltpu.SemaphoreType.DMA((2,))],
        ),
        out_shape=jax.ShapeDtypeStruct((t, D_MODEL), F32),
        compiler_params=_params("arbitrary"),
        name="moe_combine",
    )(pos, x2, final_g, ys)


def kernel(x, mem, norm_mix_g, w_in, b_forget, b_glu, w_dw, b_dw, conv_ln_g, conv_ln_b, attn_out_g, w_out,
           norm_mem_g, mem_norm_g, w_mq, w_mkv, w_mo, norm_ffn_g, w_route_group, b_route_group,
           w_route_expert, b_route_expert, w_gate, w_up, w_down, final_g):
    batch, seq, _ = x.shape
    n_mem = mem.shape[1]
    assert norm_mix_g.shape[0] == 1, "single-layer block"
    t = batch * seq
    xs = x.reshape(t, D_MODEL)
    mem2d = mem.reshape(batch * n_mem, D_MODEL)
    row = lambda v: v.reshape(1, -1)

    wi = w_in[0]
    w_packed = jnp.concatenate(
        [wi[:, :3 * D_ATTN], jnp.pad(wi[:, 3 * D_ATTN:3 * D_ATTN + ATTN_HEADS], ((0, 0), (0, LANES - ATTN_HEADS))),
         wi[:, 3 * D_ATTN + ATTN_HEADS:]], axis=1).astype(BF16)
    bf_pad = jnp.pad(b_forget[0], (0, LANES - ATTN_HEADS)).reshape(1, LANES)
    q, k, v, c, z = _inproj(xs, row(norm_mix_g[0]), w_packed, bf_pad, row(b_glu[0]), batch, seq)
    attn = _fox_attn(q, k, v, c, batch, seq)
    w_dw_pad = jnp.pad(w_dw[0], ((0, CONV_HALO - CONV_WIDTH), (0, 0)))
    conv = _conv(z, w_dw_pad, row(b_dw[0]), row(conv_ln_g[0]), row(conv_ln_b[0]), batch, seq)
    kmem, vmem = _mem_kv(mem2d, row(mem_norm_g[0]), w_mkv[0].astype(BF16), batch, n_mem)

    w_route = jnp.concatenate(
        [w_route_group[0], w_route_expert[0].transpose(1, 0, 2).reshape(D_MODEL, N_EXPERTS)], axis=1)
    w_route = jnp.pad(w_route, ((0, 0), (0, LANES - N_GROUPS - N_EXPERTS)))
    w_route_hi = w_route.astype(BF16)
    w_route = jnp.stack([w_route_hi, (w_route - w_route_hi.astype(F32)).astype(BF16)])
    b_route = jnp.pad(jnp.concatenate([b_route_group[0], b_route_expert[0].reshape(-1)]),
                      (0, LANES - N_GROUPS - N_EXPERTS)).reshape(1, LANES)
    x2, rows, rb, counts = _mid(xs, attn, conv, row(attn_out_g[0]), w_out[0].astype(BF16), row(norm_mem_g[0]),
                                w_mq[0].astype(BF16), kmem, vmem, w_mo[0].astype(BF16), row(norm_ffn_g[0]),
                                w_route, b_route, batch, seq, n_mem)

    n_tiles = t // TMX + N_BUCKETS
    pos8, tinfo = _moe_plan(counts, rb, n_tiles)
    pos = pos8[0]
    sorted_rows = _moe_dispatch(pos, rows, n_tiles * TMX)
    ys = _moe_experts(tinfo[:, 0], tinfo[:, 1], tinfo[:1, 2], sorted_rows,
                      w_gate[0].astype(BF16), w_up[0].astype(BF16), w_down[0].astype(BF16))
    out = _moe_combine(pos, x2, row(final_g), ys)
    return out.reshape(batch, seq, D_MODEL)
```

```python
import functools

import jax
import jax.numpy as jnp
import numpy as np
from jax import lax
from jax.experimental import pallas as pl
from jax.experimental.pallas import tpu as pltpu

F32 = jnp.float32
BF16 = jnp.bfloat16
I32 = jnp.int32

D_MODEL = 1024
ATTN_HEADS = 8
HEAD_DIM = 64
D_ATTN = ATTN_HEADS * HEAD_DIM
D_CONV = D_MODEL - D_ATTN
CONV_WIDTH = 31
MEM_HEADS = 4
MEM_HEAD_DIM = D_MODEL // MEM_HEADS
N_GROUPS = 4
EXPERTS_PER_GROUP = 8
N_EXPERTS = N_GROUPS * EXPERTS_PER_GROUP
D_EXPERT = D_MODEL // 2
EPS = 1e-6

LANES = 128
SUBLANES = 8
HEAD_PAIRS = ATTN_HEADS // 2
COL_Q = 0
COL_K = COL_Q + D_ATTN
COL_V = COL_K + D_ATTN
COL_F = COL_V + D_ATTN
COL_C = COL_F + LANES
D_IN_PACKED = COL_C + 2 * D_CONV
ROUTE_OFF = N_GROUPS
PAIRS_PER_GROUP = EXPERTS_PER_GROUP * (EXPERTS_PER_GROUP - 1) // 2
N_BUCKETS = N_GROUPS * PAIRS_PER_GROUP
D_ROW = D_MODEL + LANES
NEG_BIG = -1e30
LOG2E = 1.4426950408889634
VMEM_LIMIT = 48 * 1024 * 1024

TM_IN = 512
TQ = 256
TS_CONV = 256
CONV_HALO = 32
CONV_ROWS = 64
TM_MID = 512
TM_PLAN = 2048
TM_DISP = 1024
TMX = 256
TM_COMB = 512
DMA_UNROLL = 8

assert N_BUCKETS <= LANES


def _rms(x, g):
    return x * lax.rsqrt(jnp.mean(x * x, axis=-1, keepdims=True) + EPS) * g


def _split3(c):
    hi = c.astype(BF16)
    r = c - hi.astype(F32)
    mid = r.astype(BF16)
    lo = (r - mid.astype(F32)).astype(BF16)
    return hi, mid, lo


def _params(*sem):
    return pltpu.CompilerParams(dimension_semantics=sem, vmem_limit_bytes=VMEM_LIMIT)


def _inproj_kernel(x_ref, g_ref, w_ref, bf_ref, bglu_ref, tri_ref,
                   q_ref, k_ref, v_ref, c_ref, z_ref, carry_ref):
    @pl.when(pl.program_id(1) == 0)
    def _():
        carry_ref[...] = jnp.zeros_like(carry_ref)

    h = _rms(x_ref[...], g_ref[...]).astype(BF16)
    q_ref[...] = jnp.dot(h, w_ref[:, COL_Q:COL_K], preferred_element_type=F32).astype(BF16)
    k_ref[...] = jnp.dot(h, w_ref[:, COL_K:COL_V], preferred_element_type=F32).astype(BF16)
    v_ref[...] = jnp.dot(h, w_ref[:, COL_V:COL_F], preferred_element_type=F32).astype(BF16)

    fl = jnp.dot(h, w_ref[:, COL_F:COL_C], preferred_element_type=F32) + bf_ref[...]
    lf = jnp.minimum(fl, 0.0) - jnp.log(1.0 + jnp.exp(-jnp.abs(fl)))
    tri = tri_ref[...]
    c = carry_ref[...]
    for part in _split3(lf):
        c = c + jnp.dot(tri, part, preferred_element_type=F32)
    carry_ref[...] = c[-1:, :]
    c_ref[...] = c[:, :ATTN_HEADS] * LOG2E

    u = jnp.dot(h, w_ref[:, COL_C:D_IN_PACKED], preferred_element_type=F32) + bglu_ref[...]
    z_ref[...] = (u[:, :D_CONV] * jax.nn.sigmoid(u[:, D_CONV:])).astype(BF16)


def _inproj(x2d, g, w_packed, bf_pad, b_glu, batch, seq):
    t = batch * seq
    nt = seq // TM_IN
    tri = jnp.tril(jnp.ones((TM_IN, TM_IN), BF16))
    tok = lambda b, i: (b * nt + i, 0)
    const = lambda b, i: (0, 0)
    return pl.pallas_call(
        _inproj_kernel,
        grid=(batch, nt),
        in_specs=[
            pl.BlockSpec((TM_IN, D_MODEL), tok),
            pl.BlockSpec((1, D_MODEL), const),
            pl.BlockSpec((D_MODEL, D_IN_PACKED), const),
            pl.BlockSpec((1, LANES), const),
            pl.BlockSpec((1, 2 * D_CONV), const),
            pl.BlockSpec((TM_IN, TM_IN), const),
        ],
        out_specs=[
            pl.BlockSpec((TM_IN, D_ATTN), tok),
            pl.BlockSpec((TM_IN, D_ATTN), tok),
            pl.BlockSpec((TM_IN, D_ATTN), tok),
            pl.BlockSpec((TM_IN, ATTN_HEADS), tok),
            pl.BlockSpec((TM_IN, D_CONV), tok),
        ],
        out_shape=[
            jax.ShapeDtypeStruct((t, D_ATTN), BF16),
            jax.ShapeDtypeStruct((t, D_ATTN), BF16),
            jax.ShapeDtypeStruct((t, D_ATTN), BF16),
            jax.ShapeDtypeStruct((t, ATTN_HEADS), F32),
            jax.ShapeDtypeStruct((t, D_CONV), BF16),
        ],
        scratch_shapes=[pltpu.VMEM((1, LANES), F32)],
        compiler_params=_params("arbitrary", "arbitrary"),
        name="inproj",
    )(x2d, g, w_packed, bf_pad, b_glu, tri)


def _attn_kernel(q_ref, k_ref, v_ref, c_ref, o_ref, ka_ref, va_ref, qa_ref, m_ref, acc_ref, s_ref, p_ref, alpha_ref):
    i = pl.program_id(1)
    seq = k_ref.shape[0]

    def halves(rows):
        lane = lax.broadcasted_iota(I32, (rows, LANES), 1)
        own = (lane < HEAD_DIM, lane >= HEAD_DIM)
        aug = (lane - HEAD_DIM, lane)
        return own, aug

    def split3_f32(c):
        return [p.astype(F32) for p in _split3(c)]

    @pl.when(i == 0)
    def _():
        own, aug = halves(seq)
        c = c_ref[...]
        for p in range(HEAD_PAIRS):
            k = k_ref[:, p * LANES:(p + 1) * LANES].astype(F32)
            v = v_ref[:, p * LANES:(p + 1) * LANES].astype(F32)
            for h in range(2):
                a = aug[h]
                hi, mid, lo = split3_f32(c[:, 2 * p + h:2 * p + h + 1])
                extra = jnp.where(a < 3, 1.0, jnp.where(a == 3, -hi, jnp.where(a == 4, -mid, jnp.where(a == 5, -lo, 0.0))))
                ka_ref[2 * p + h] = jnp.where(own[h], k, extra).astype(BF16)
                va_ref[2 * p + h] = jnp.where(own[h], v, 1.0).astype(BF16)

    row0 = pl.multiple_of(i * TQ, TQ)
    own, aug = halves(TQ)
    cq = c_ref[pl.ds(row0, TQ), :]
    for p in range(HEAD_PAIRS):
        q = q_ref[:, p * LANES:(p + 1) * LANES].astype(F32) * (HEAD_DIM ** -0.5 * LOG2E)
        for h in range(2):
            a = aug[h]
            hi, mid, lo = split3_f32(cq[:, 2 * p + h:2 * p + h + 1])
            extra = jnp.where(a == 0, hi, jnp.where(a == 1, mid, jnp.where(a == 2, lo, jnp.where(a < 6, 1.0, 0.0))))
            qa_ref[2 * p + h] = jnp.where(own[h], q, extra).astype(BF16)
    m_ref[...] = jnp.full(m_ref.shape, NEG_BIG, F32)
    acc_ref[...] = jnp.zeros_like(acc_ref)

    nt_dims = (((1,), (1,)), ((), ()))

    def chunk(j, masked):
        k0 = pl.multiple_of(j * TQ, TQ)
        for hh in range(ATTN_HEADS):
            s_ref[hh] = lax.dot_general(qa_ref[hh], ka_ref[hh, pl.ds(k0, TQ), :], nt_dims,
                                        preferred_element_type=F32)
        for hh in range(ATTN_HEADS):
            s = s_ref[hh]
            if masked:
                r = lax.broadcasted_iota(I32, (TQ, TQ), 0)
                cc = lax.broadcasted_iota(I32, (TQ, TQ), 1)
                s = jnp.where(r >= cc, s, NEG_BIG)
            m_old = m_ref[hh]
            m_new = jnp.maximum(m_old, jnp.max(s, axis=-1, keepdims=True))
            p_ref[hh] = jnp.concatenate(
                [jnp.exp2(s[:, n * LANES:(n + 1) * LANES] - m_new) for n in range(TQ // LANES)], axis=1).astype(BF16)
            alpha_ref[hh] = jnp.exp2(m_old - m_new)
            m_ref[hh] = m_new
        for hh in range(ATTN_HEADS):
            pv = jnp.dot(p_ref[hh], va_ref[hh, pl.ds(k0, TQ), :], preferred_element_type=F32)
            acc_ref[hh] = alpha_ref[hh] * acc_ref[hh] + pv

    def body(j, carry):
        chunk(j, False)
        return carry

    lax.fori_loop(0, i, body, 0)
    chunk(i, True)
    for p in range(HEAD_PAIRS):
        outs = [acc_ref[hh] / pltpu.roll(acc_ref[hh], HEAD_DIM, 1) for hh in (2 * p, 2 * p + 1)]
        o_ref[:, p * LANES:(p + 1) * LANES] = jnp.where(own[0], outs[0], outs[1]).astype(BF16)


def _fox_attn(q, k, v, c, batch, seq):
    t = batch * seq
    nq = seq // TQ
    return pl.pallas_call(
        _attn_kernel,
        grid=(batch, nq),
        in_specs=[
            pl.BlockSpec((TQ, D_ATTN), lambda b, i: (b * nq + i, 0)),
            pl.BlockSpec((seq, D_ATTN), lambda b, i: (b, 0)),
            pl.BlockSpec((seq, D_ATTN), lambda b, i: (b, 0)),
            pl.BlockSpec((seq, ATTN_HEADS), lambda b, i: (b, 0)),
        ],
        out_specs=pl.BlockSpec((TQ, D_ATTN), lambda b, i: (b * nq + i, 0)),
        out_shape=jax.ShapeDtypeStruct((t, D_ATTN), BF16),
        scratch_shapes=[
            pltpu.VMEM((ATTN_HEADS, seq, LANES), BF16),
            pltpu.VMEM((ATTN_HEADS, seq, LANES), BF16),
            pltpu.VMEM((ATTN_HEADS, TQ, LANES), BF16),
            pltpu.VMEM((ATTN_HEADS, TQ, LANES), F32),
            pltpu.VMEM((ATTN_HEADS, TQ, LANES), F32),
            pltpu.VMEM((ATTN_HEADS, TQ, TQ), F32),
            pltpu.VMEM((ATTN_HEADS, TQ, TQ), BF16),
            pltpu.VMEM((ATTN_HEADS, TQ, LANES), F32),
        ],
        compiler_params=_params("arbitrary", "arbitrary"),
        name="fox_attn",
    )(q, k, v, c)


def _conv_kernel(z_ref, w_ref, b_ref, g_ref, beta_ref, o_ref, buf_ref, sh_ref):
    @pl.when(pl.program_id(1) == 0)
    def _():
        buf_ref[0:CONV_HALO, :] = jnp.zeros((CONV_HALO, D_CONV), F32)

    buf_ref[CONV_HALO:, :] = z_ref[...].astype(F32)
    span = TS_CONV + CONV_HALO - SUBLANES
    for s in range(1, SUBLANES):
        sh_ref[s, 0:span, :] = buf_ref[s:s + span, :]
    first = CONV_HALO - (CONV_WIDTH - 1)
    for r in range(TS_CONV // CONV_ROWS):
        acc = jnp.zeros((CONV_ROWS, D_CONV), F32)
        for j in range(CONV_WIDTH):
            s = (first + j) % SUBLANES
            lo = r * CONV_ROWS + first + j - s
            tap = sh_ref[s, lo:lo + CONV_ROWS, :] if s else buf_ref[lo:lo + CONV_ROWS, :]
            acc = acc + tap * w_ref[j:j + 1, :]
        y = acc + b_ref[...]
        mu = jnp.mean(y, axis=-1, keepdims=True)
        yc = y - mu
        y = yc * lax.rsqrt(jnp.mean(yc * yc, axis=-1, keepdims=True) + EPS) * g_ref[...] + beta_ref[...]
        o_ref[r * CONV_ROWS:(r + 1) * CONV_ROWS, :] = (y * jax.nn.sigmoid(y)).astype(BF16)
    buf_ref[0:CONV_HALO, :] = buf_ref[TS_CONV:TS_CONV + CONV_HALO, :]


def _conv(z, w_dw_pad, b_dw, ln_g, ln_b, batch, seq):
    t = batch * seq
    ns = seq // TS_CONV
    tok = lambda b, i: (b * ns + i, 0)
    const = lambda b, i: (0, 0)
    return pl.pallas_call(
        _conv_kernel,
        grid=(batch, ns),
        in_specs=[
            pl.BlockSpec((TS_CONV, D_CONV), tok),
            pl.BlockSpec((CONV_HALO, D_CONV), const),
            pl.BlockSpec((1, D_CONV), const),
            pl.BlockSpec((1, D_CONV), const),
            pl.BlockSpec((1, D_CONV), const),
        ],
        out_specs=pl.BlockSpec((TS_CONV, D_CONV), tok),
        out_shape=jax.ShapeDtypeStruct((t, D_CONV), BF16),
        scratch_shapes=[pltpu.VMEM((TS_CONV + CONV_HALO, D_CONV), F32),
                        pltpu.VMEM((SUBLANES, TS_CONV + CONV_HALO, D_CONV), F32)],
        compiler_params=_params("arbitrary", "arbitrary"),
        name="conv",
    )(z, w_dw_pad, b_dw, ln_g, ln_b)


def _memkv_kernel(m_ref, g_ref, w_ref, k_ref, v_ref):
    h = _rms(m_ref[...], g_ref[...]).astype(BF16)
    k = jnp.dot(h, w_ref[:, :D_MODEL], preferred_element_type=F32)
    k_ref[...] = (k * (MEM_HEAD_DIM ** -0.5)).astype(BF16)
    v_ref[...] = jnp.dot(h, w_ref[:, D_MODEL:], preferred_element_type=F32).astype(BF16)


def _mem_kv(mem2d, g, w_mkv, batch, n_mem):
    return pl.pallas_call(
        _memkv_kernel,
        grid=(batch,),
        in_specs=[
            pl.BlockSpec((n_mem, D_MODEL), lambda b: (b, 0)),
            pl.BlockSpec((1, D_MODEL), lambda b: (0, 0)),
            pl.BlockSpec((D_MODEL, 2 * D_MODEL), lambda b: (0, 0)),
        ],
        out_specs=[pl.BlockSpec((n_mem, D_MODEL), lambda b: (b, 0))] * 2,
        out_shape=[jax.ShapeDtypeStruct((batch * n_mem, D_MODEL), BF16)] * 2,
        compiler_params=_params("arbitrary"),
        name="mem_kv",
    )(mem2d, g, w_mkv)


def _mid_kernel(x_ref, a_ref, cv_ref, ag_ref, wout_ref, gm_ref, wmq_ref, km_ref, vm_ref, wmo_ref,
                gf_ref, wr_ref, br_ref, tri_ref,
                x2_ref, row_ref, rb_ref, cnt_ref):
    @pl.when(pl.program_id(0) == 0)
    def _():
        cnt_ref[...] = jnp.zeros_like(cnt_ref)

    a = _rms(a_ref[...].astype(F32), ag_ref[...]).astype(BF16)
    mix = jnp.dot(a, wout_ref[:D_ATTN, :], preferred_element_type=F32)
    mix = mix + jnp.dot(cv_ref[...], wout_ref[D_ATTN:, :], preferred_element_type=F32)
    x1 = x_ref[...] + mix

    h2 = _rms(x1, gm_ref[...]).astype(BF16)
    qm = jnp.dot(h2, wmq_ref[...], preferred_element_type=F32).astype(BF16)
    heads = []
    for hh in range(MEM_HEADS):
        sl = slice(hh * MEM_HEAD_DIM, (hh + 1) * MEM_HEAD_DIM)
        s = lax.dot_general(qm[:, sl], km_ref[:, sl], (((1,), (1,)), ((), ())), preferred_element_type=F32)
        p = jnp.exp(s - jnp.max(s, axis=-1, keepdims=True))
        l = jnp.sum(p, axis=-1, keepdims=True)
        o = jnp.dot(p.astype(BF16), vm_ref[:, sl], preferred_element_type=F32) / l
        heads.append(o.astype(BF16))
    o = jnp.concatenate(heads, axis=-1)
    x2 = x1 + jnp.dot(o, wmo_ref[...], preferred_element_type=F32)
    x2_ref[...] = x2

    h3 = _rms(x2, gf_ref[...])
    h_hi = h3.astype(BF16)
    h_lo = (h3 - h_hi.astype(F32)).astype(BF16)
    logits = (jnp.dot(h_hi, wr_ref[0], preferred_element_type=F32)
              + jnp.dot(h_hi, wr_ref[1], preferred_element_type=F32)
              + jnp.dot(h_lo, wr_ref[0], preferred_element_type=F32)) + br_ref[...]
    lane = lax.broadcasted_iota(I32, logits.shape, 1)

    def top(vals):
        m = jnp.max(vals, axis=-1, keepdims=True)
        idx = jnp.min(jnp.where(vals == m, lane, LANES), axis=-1, keepdims=True)
        return m, idx

    gl = jnp.where(lane < N_GROUPS, logits, NEG_BIG)
    gmax, gidx = top(gl)
    g_top = 1.0 / jnp.sum(jnp.exp(gl - gmax), axis=-1, keepdims=True)
    first = ROUTE_OFF + EXPERTS_PER_GROUP * gidx
    el = jnp.where((lane >= first) & (lane < first + EXPERTS_PER_GROUP), logits, NEG_BIG)
    m1, i1 = top(el)
    m2, i2 = top(jnp.where(lane == i1, NEG_BIG, el))
    e2 = jnp.exp(m2 - m1)
    w1 = g_top / (1.0 + e2)
    w2 = g_top * e2 / (1.0 + e2)

    swap = i2 < i1
    a = jnp.minimum(i1, i2) - first
    b = jnp.maximum(i1, i2) - first
    pair = a * EXPERTS_PER_GROUP - ((a * (a + 1)) >> 1) + (b - a - 1)
    bucket = gidx * PAIRS_PER_GROUP + pair
    wa = jnp.where(swap, w2, w1)
    wb = jnp.where(swap, w1, w2)
    row_ref[:, :D_MODEL] = h3
    row_ref[:, D_MODEL:] = jnp.where(lane == 0, wa, jnp.where(lane == 1, wb, 0.0))

    hit = lane == bucket
    cum = cnt_ref[...] + jnp.dot(tri_ref[...], jnp.where(hit, 1.0, 0.0).astype(BF16), preferred_element_type=F32)
    rank = jnp.sum(jnp.where(hit, cum, 0.0), axis=-1, keepdims=True) - 1.0
    cnt_ref[...] = cum[-1:, :]
    l8 = lax.broadcasted_iota(I32, (TM_MID, SUBLANES), 1)
    rb_ref[...] = jnp.where(l8 == 0, bucket.astype(F32), jnp.where(l8 == 1, rank, 0.0))


def _mid(x2d, attn, conv, ag, w_out, gm, w_mq, kmem, vmem, w_mo, gf, w_route, b_route, batch, seq, n_mem):
    t = batch * seq
    per_b = seq // TM_MID
    tri = jnp.tril(jnp.ones((TM_MID, TM_MID), BF16))
    tok = lambda i: (i, 0)
    const = lambda i: (0, 0)
    memb = lambda i: (i // per_b, 0)
    return pl.pallas_call(
        _mid_kernel,
        grid=(t // TM_MID,),
        in_specs=[
            pl.BlockSpec((TM_MID, D_MODEL), tok),
            pl.BlockSpec((TM_MID, D_ATTN), tok),
            pl.BlockSpec((TM_MID, D_CONV), tok),
            pl.BlockSpec((1, D_ATTN), const),
            pl.BlockSpec((D_MODEL, D_MODEL), const),
            pl.BlockSpec((1, D_MODEL), const),
            pl.BlockSpec((D_MODEL, D_MODEL), const),
            pl.BlockSpec((n_mem, D_MODEL), memb),
            pl.BlockSpec((n_mem, D_MODEL), memb),
            pl.BlockSpec((D_MODEL, D_MODEL), const),
            pl.BlockSpec((1, D_MODEL), const),
            pl.BlockSpec((2, D_MODEL, LANES), lambda i: (0, 0, 0)),
            pl.BlockSpec((1, LANES), const),
            pl.BlockSpec((TM_MID, TM_MID), const),
        ],
        out_specs=[
            pl.BlockSpec((TM_MID, D_MODEL), tok),
            pl.BlockSpec((TM_MID, D_ROW), tok),
            pl.BlockSpec((TM_MID, SUBLANES), tok),
            pl.BlockSpec((1, LANES), const),
        ],
        out_shape=[
            jax.ShapeDtypeStruct((t, D_MODEL), F32),
            jax.ShapeDtypeStruct((t, D_ROW), F32),
            jax.ShapeDtypeStruct((t, SUBLANES), F32),
            jax.ShapeDtypeStruct((1, LANES), F32),
        ],
        compiler_params=_params("arbitrary"),
        name="mid",
    )(x2d, attn, conv, ag, w_out, gm, w_mq, kmem, vmem, w_mo, gf, w_route, b_route, tri)


def _bucket_expert_tables():
    ea = np.zeros((1, LANES), np.float32)
    eb = np.zeros((1, LANES), np.float32)
    for g in range(N_GROUPS):
        n = 0
        for a in range(EXPERTS_PER_GROUP):
            for b in range(a + 1, EXPERTS_PER_GROUP):
                ea[0, g * PAIRS_PER_GROUP + n] = g * EXPERTS_PER_GROUP + a
                eb[0, g * PAIRS_PER_GROUP + n] = g * EXPERTS_PER_GROUP + b
                n += 1
    return jnp.asarray(ea), jnp.asarray(eb)


def _plan_kernel(cnt_ref, rb_ref, ea_ref, eb_ref, triu_ref, pos_ref, tinfo_ref):
    hi = lax.Precision.HIGHEST
    cnt = cnt_ref[...]
    padded = jnp.floor((cnt + (TMX - 1)) * (1.0 / TMX)) * TMX
    end = jnp.dot(padded, triu_ref[...], preferred_element_type=F32, precision=hi)
    base = end - padded

    @pl.when(pl.program_id(0) == 0)
    def _():
        n_tiles = tinfo_ref.shape[0]
        lane = lax.broadcasted_iota(I32, (n_tiles, LANES), 1)
        start = lax.broadcasted_iota(I32, (n_tiles, LANES), 0).astype(F32) * TMX
        done = (lane < N_BUCKETS) & (end <= start)
        tb = jnp.sum(jnp.where(done, 1, 0), axis=-1, keepdims=True)
        ea = jnp.sum(jnp.where(lane == tb, ea_ref[...], 0.0), axis=-1, keepdims=True)
        eb = jnp.sum(jnp.where(lane == tb, eb_ref[...], 0.0), axis=-1, keepdims=True)
        used = jnp.sum(jnp.where(lane == N_BUCKETS - 1, end, 0.0), axis=-1, keepdims=True) * (1.0 / TMX)
        l8 = lax.broadcasted_iota(I32, (n_tiles, SUBLANES), 1)
        tinfo_ref[...] = jnp.where(l8 == 0, ea, jnp.where(l8 == 1, eb, jnp.where(l8 == 2, used, 0.0))).astype(I32)

    rb = rb_ref[...]
    lane = lax.broadcasted_iota(I32, (rb.shape[0], LANES), 1)
    hit = lane == rb[:, 0:1].astype(I32)
    dest = jnp.where(hit, base + rb[:, 1:2], 0.0)
    pos = lax.dot_general(jnp.ones((SUBLANES, LANES), F32), dest, (((1,), (1,)), ((), ())),
                          preferred_element_type=F32, precision=hi)
    pos_ref[...] = pos.astype(I32)


def _moe_plan(counts, rb, n_tiles):
    t = rb.shape[0]
    ea, eb = _bucket_expert_tables()
    triu = jnp.triu(jnp.ones((LANES, LANES), F32))
    const = lambda i: (0, 0)
    return pl.pallas_call(
        _plan_kernel,
        grid=(t // TM_PLAN,),
        in_specs=[
            pl.BlockSpec((1, LANES), const),
            pl.BlockSpec((TM_PLAN, SUBLANES), lambda i: (i, 0)),
            pl.BlockSpec((1, LANES), const),
            pl.BlockSpec((1, LANES), const),
            pl.BlockSpec((LANES, LANES), const),
        ],
        out_specs=[
            pl.BlockSpec((SUBLANES, TM_PLAN), lambda i: (0, i)),
            pl.BlockSpec((n_tiles, SUBLANES), const),
        ],
        out_shape=[
            jax.ShapeDtypeStruct((SUBLANES, t), I32),
            jax.ShapeDtypeStruct((n_tiles, SUBLANES), I32),
        ],
        compiler_params=_params("arbitrary"),
        name="moe_plan",
    )(counts, rb, ea, eb, triu)


def _dispatch_kernel(pos_ref, rows_ref, init_hbm, xs_hbm, sem):
    del init_hbm
    t0 = pl.program_id(0) * TM_DISP

    def issue(r, carry):
        pltpu.make_async_copy(rows_ref.at[pl.ds(r, 1), :], xs_hbm.at[pl.ds(pos_ref[t0 + r], 1), :], sem).start()
        return carry

    lax.fori_loop(0, TM_DISP, issue, 0, unroll=DMA_UNROLL)
    pltpu.make_async_copy(rows_ref, xs_hbm.at[pl.ds(0, TM_DISP), :], sem).wait()


def _moe_dispatch(pos, rows, n_rows):
    t = rows.shape[0]
    init = jnp.zeros((n_rows, D_ROW), F32)
    return pl.pallas_call(
        _dispatch_kernel,
        grid_spec=pltpu.PrefetchScalarGridSpec(
            num_scalar_prefetch=1,
            grid=(t // TM_DISP,),
            in_specs=[pl.BlockSpec((TM_DISP, D_ROW), lambda i, pos: (i, 0)), pl.BlockSpec(memory_space=pl.ANY)],
            out_specs=pl.BlockSpec(memory_space=pl.ANY),
            scratch_shapes=[pltpu.SemaphoreType.DMA],
        ),
        out_shape=jax.ShapeDtypeStruct((n_rows, D_ROW), F32),
        input_output_aliases={2: 0},
        compiler_params=_params("arbitrary"),
        name="moe_dispatch",
    )(pos, rows, init)


def _experts_kernel(ea_ref, eb_ref, used_ref, xs_ref, wga_ref, wua_ref, wda_ref, wgb_ref, wub_ref, wdb_ref, ys_ref):
    del ea_ref, eb_ref

    @pl.when(pl.program_id(0) < used_ref[0])
    def _():
        x = xs_ref[:, :D_MODEL].astype(BF16)
        gates = xs_ref[:, D_MODEL:]
        y = None
        for col, wg_ref, wu_ref, wd_ref in ((0, wga_ref, wua_ref, wda_ref), (1, wgb_ref, wub_ref, wdb_ref)):
            hg = jnp.dot(x, wg_ref[...], preferred_element_type=F32)
            hu = jnp.dot(x, wu_ref[...], preferred_element_type=F32)
            act = (hg * jax.nn.sigmoid(hg) * hu * gates[:, col:col + 1]).astype(BF16)
            part = jnp.dot(act, wd_ref[...], preferred_element_type=F32)
            y = part if y is None else y + part
        ys_ref[...] = y

    @pl.when(pl.program_id(0) >= used_ref[0])
    def _():
        ys_ref[...] = jnp.zeros_like(ys_ref)


def _moe_experts(tile_ea, tile_eb, used, xs, w_gate, w_up, w_down):
    n_rows = xs.shape[0]
    n_tiles = n_rows // TMX
    last = lambda r, used: jnp.maximum(jnp.minimum(r, used[0] - 1), 0)
    row = lambda r, ea, eb, used: (last(r, used), 0)
    wa = lambda r, ea, eb, used: (ea[last(r, used)], 0, 0)
    wb = lambda r, ea, eb, used: (eb[last(r, used)], 0, 0)
    return pl.pallas_call(
        _experts_kernel,
        grid_spec=pltpu.PrefetchScalarGridSpec(
            num_scalar_prefetch=3,
            grid=(n_tiles,),
            in_specs=[
                pl.BlockSpec((TMX, D_ROW), row),
                pl.BlockSpec((None, D_MODEL, D_EXPERT), wa),
                pl.BlockSpec((None, D_MODEL, D_EXPERT), wa),
                pl.BlockSpec((None, D_EXPERT, D_MODEL), wa),
                pl.BlockSpec((None, D_MODEL, D_EXPERT), wb),
                pl.BlockSpec((None, D_MODEL, D_EXPERT), wb),
                pl.BlockSpec((None, D_EXPERT, D_MODEL), wb),
            ],
            out_specs=pl.BlockSpec((TMX, D_MODEL), lambda r, ea, eb, used: (r, 0)),
        ),
        out_shape=jax.ShapeDtypeStruct((n_rows, D_MODEL), F32),
        compiler_params=_params("arbitrary"),
        name="moe_experts",
    )(tile_ea, tile_eb, used, xs, w_gate, w_up, w_down, w_gate, w_up, w_down)


def _combine_kernel(pos_ref, x2_ref, fg_ref, ys_hbm, o_ref, ybuf, sem):
    i = pl.program_id(0)
    n = pl.num_programs(0)

    def gather(step, slot):
        t0 = step * TM_COMB

        def issue(r, carry):
            pltpu.make_async_copy(ys_hbm.at[pl.ds(pos_ref[t0 + r], 1), :], ybuf.at[slot, pl.ds(r, 1), :],
                                  sem.at[slot]).start()
            return carry

        lax.fori_loop(0, TM_COMB, issue, 0, unroll=DMA_UNROLL)

    @pl.when(i == 0)
    def _():
        gather(0, 0)

    slot = lax.rem(i, 2)

    @pl.when(i + 1 < n)
    def _():
        gather(i + 1, 1 - slot)

    pltpu.make_async_copy(ys_hbm.at[pl.ds(0, TM_COMB), :], ybuf.at[slot], sem.at[slot]).wait()
    o_ref[...] = _rms(x2_ref[...] + ybuf[slot], fg_ref[...])


def _moe_combine(pos, x2, final_g, ys):
    t = x2.shape[0]
    return pl.pallas_call(
        _combine_kernel,
        grid_spec=pltpu.PrefetchScalarGridSpec(
            num_scalar_prefetch=1,
            grid=(t // TM_COMB,),
            in_specs=[
                pl.BlockSpec((TM_COMB, D_MODEL), lambda i, pos: (i, 0)),
                pl.BlockSpec((1, D_MODEL), lambda i, pos: (0, 0)),
                pl.BlockSpec(memory_space=pl.ANY),
            ],
            out_specs=pl.BlockSpec((TM_COMB, D_MODEL), lambda i, pos: (i, 0)),
            scratch_shapes=[pltpu.VMEM((2, TM_COMB, D_MODEL), F32), pltpu.SemaphoreType.DMA((2,))],
        ),
        out_shape=jax.ShapeDtypeStruct((t, D_MODEL), F32),
        compiler_params=_params("arbitrary"),
        name="moe_combine",
    )(pos, x2, final_g, ys)


def kernel(x, mem, norm_mix_g, w_in, b_forget, b_glu, w_dw, b_dw, conv_ln_g, conv_ln_b, attn_out_g, w_out,
           norm_mem_g, mem_norm_g, w_mq, w_mkv, w_mo, norm_ffn_g, w_route_group, b_route_group,
           w_route_expert, b_route_expert, w_gate, w_up, w_down, final_g):
    batch, seq, _ = x.shape
    n_mem = mem.shape[1]
    assert norm_mix_g.shape[0] == 1, "single-layer block"
    t = batch * seq
    xs = x.reshape(t, D_MODEL)
    mem2d = mem.reshape(batch * n_mem, D_MODEL)
    row = lambda v: v.reshape(1, -1)

    wi = w_in[0]
    w_packed = jnp.concatenate(
        [wi[:, :3 * D_ATTN], jnp.pad(wi[:, 3 * D_ATTN:3 * D_ATTN + ATTN_HEADS], ((0, 0), (0, LANES - ATTN_HEADS))),
         wi[:, 3 * D_ATTN + ATTN_HEADS:]], axis=1).astype(BF16)
    bf_pad = jnp.pad(b_forget[0], (0, LANES - ATTN_HEADS)).reshape(1, LANES)
    q, k, v, c, z = _inproj(xs, row(norm_mix_g[0]), w_packed, bf_pad, row(b_glu[0]), batch, seq)
    attn = _fox_attn(q, k, v, c, batch, seq)
    w_dw_pad = jnp.pad(w_dw[0], ((0, CONV_HALO - CONV_WIDTH), (0, 0)))
    conv = _conv(z, w_dw_pad, row(b_dw[0]), row(conv_ln_g[0]), row(conv_ln_b[0]), batch, seq)
    kmem, vmem = _mem_kv(mem2d, row(mem_norm_g[0]), w_mkv[0].astype(BF16), batch, n_mem)

    w_route = jnp.concatenate(
        [w_route_group[0], w_route_expert[0].transpose(1, 0, 2).reshape(D_MODEL, N_EXPERTS)], axis=1)
    w_route = jnp.pad(w_route, ((0, 0), (0, LANES - N_GROUPS - N_EXPERTS)))
    w_route_hi = w_route.astype(BF16)
    w_route = jnp.stack([w_route_hi, (w_route - w_route_hi.astype(F32)).astype(BF16)])
    b_route = jnp.pad(jnp.concatenate([b_route_group[0], b_route_expert[0].reshape(-1)]),
                      (0, LANES - N_GROUPS - N_EXPERTS)).reshape(1, LANES)
    x2, rows, rb, counts = _mid(xs, attn, conv, row(attn_out_g[0]), w_out[0].astype(BF16), row(norm_mem_g[0]),
                                w_mq[0].astype(BF16), kmem, vmem, w_mo[0].astype(BF16), row(norm_ffn_g[0]),
                                w_route, b_route, batch, seq, n_mem)

    n_tiles = t // TMX + N_BUCKETS
    pos8, tinfo = _moe_plan(counts, rb, n_tiles)
    pos = pos8[0]
    sorted_rows = _moe_dispatch(pos, rows, n_tiles * TMX)
    ys = _moe_experts(tinfo[:, 0], tinfo[:, 1], tinfo[:1, 2], sorted_rows,
                      w_gate[0].astype(BF16), w_up[0].astype(BF16), w_down[0].astype(BF16))
    out = _moe_combine(pos, x2, row(final_g), ys)
    return out.reshape(batch, seq, D_MODEL)
```

```python
import functools

import jax
import jax.numpy as jnp
import numpy as np
from jax import lax
from jax.experimental import pallas as pl
from jax.experimental.pallas import tpu as pltpu

F32 = jnp.float32
BF16 = jnp.bfloat16
I32 = jnp.int32

D_MODEL = 1024
ATTN_HEADS = 8
HEAD_DIM = 64
D_ATTN = ATTN_HEADS * HEAD_DIM
D_CONV = D_MODEL - D_ATTN
CONV_WIDTH = 31
MEM_HEADS = 4
MEM_HEAD_DIM = D_MODEL // MEM_HEADS
N_GROUPS = 4
EXPERTS_PER_GROUP = 8
N_EXPERTS = N_GROUPS * EXPERTS_PER_GROUP
D_EXPERT = D_MODEL // 2
EPS = 1e-6

LANES = 128
SUBLANES = 8
HEAD_PAIRS = ATTN_HEADS // 2
COL_Q = 0
COL_K = COL_Q + D_ATTN
COL_V = COL_K + D_ATTN
COL_F = COL_V + D_ATTN
COL_C = COL_F + LANES
D_IN_PACKED = COL_C + 2 * D_CONV
ROUTE_OFF = N_GROUPS
PAIRS_PER_GROUP = EXPERTS_PER_GROUP * (EXPERTS_PER_GROUP - 1) // 2
N_BUCKETS = N_GROUPS * PAIRS_PER_GROUP
D_ROW = D_MODEL + LANES
NEG_BIG = -1e30
LOG2E = 1.4426950408889634
VMEM_LIMIT = 48 * 1024 * 1024

TM_IN = 512
TQ = 256
TS_CONV = 256
CONV_HALO = 32
CONV_ROWS = 64
TM_MID = 512
TM_PLAN = 2048
TM_DISP = 512
TMX = 256
TM_COMB = 512

assert N_BUCKETS <= LANES


def _rms(x, g):
    return x * lax.rsqrt(jnp.mean(x * x, axis=-1, keepdims=True) + EPS) * g


def _split3(c):
    hi = c.astype(BF16)
    r = c - hi.astype(F32)
    mid = r.astype(BF16)
    lo = (r - mid.astype(F32)).astype(BF16)
    return hi, mid, lo


def _params(*sem):
    return pltpu.CompilerParams(dimension_semantics=sem, vmem_limit_bytes=VMEM_LIMIT)


def _inproj_kernel(x_ref, g_ref, w_ref, bf_ref, bglu_ref, tri_ref,
                   q_ref, k_ref, v_ref, c_ref, z_ref, carry_ref):
    @pl.when(pl.program_id(1) == 0)
    def _():
        carry_ref[...] = jnp.zeros_like(carry_ref)

    h = _rms(x_ref[...], g_ref[...]).astype(BF16)
    q_ref[...] = jnp.dot(h, w_ref[:, COL_Q:COL_K], preferred_element_type=F32).astype(BF16)
    k_ref[...] = jnp.dot(h, w_ref[:, COL_K:COL_V], preferred_element_type=F32).astype(BF16)
    v_ref[...] = jnp.dot(h, w_ref[:, COL_V:COL_F], preferred_element_type=F32).astype(BF16)

    fl = jnp.dot(h, w_ref[:, COL_F:COL_C], preferred_element_type=F32) + bf_ref[...]
    lf = jnp.minimum(fl, 0.0) - jnp.log(1.0 + jnp.exp(-jnp.abs(fl)))
    tri = tri_ref[...]
    c = carry_ref[...]
    for part in _split3(lf):
        c = c + jnp.dot(tri, part, preferred_element_type=F32)
    carry_ref[...] = c[-1:, :]
    c_ref[...] = c[:, :ATTN_HEADS] * LOG2E

    u = jnp.dot(h, w_ref[:, COL_C:D_IN_PACKED], preferred_element_type=F32) + bglu_ref[...]
    z_ref[...] = (u[:, :D_CONV] * jax.nn.sigmoid(u[:, D_CONV:])).astype(BF16)


def _inproj(x2d, g, w_packed, bf_pad, b_glu, batch, seq):
    t = batch * seq
    nt = seq // TM_IN
    tri = jnp.tril(jnp.ones((TM_IN, TM_IN), BF16))
    tok = lambda b, i: (b * nt + i, 0)
    const = lambda b, i: (0, 0)
    return pl.pallas_call(
        _inproj_kernel,
        grid=(batch, nt),
        in_specs=[
            pl.BlockSpec((TM_IN, D_MODEL), tok),
            pl.BlockSpec((1, D_MODEL), const),
            pl.BlockSpec((D_MODEL, D_IN_PACKED), const),
            pl.BlockSpec((1, LANES), const),
            pl.BlockSpec((1, 2 * D_CONV), const),
            pl.BlockSpec((TM_IN, TM_IN), const),
        ],
        out_specs=[
            pl.BlockSpec((TM_IN, D_ATTN), tok),
            pl.BlockSpec((TM_IN, D_ATTN), tok),
            pl.BlockSpec((TM_IN, D_ATTN), tok),
            pl.BlockSpec((TM_IN, ATTN_HEADS), tok),
            pl.BlockSpec((TM_IN, D_CONV), tok),
        ],
        out_shape=[
            jax.ShapeDtypeStruct((t, D_ATTN), BF16),
            jax.ShapeDtypeStruct((t, D_ATTN), BF16),
            jax.ShapeDtypeStruct((t, D_ATTN), BF16),
            jax.ShapeDtypeStruct((t, ATTN_HEADS), F32),
            jax.ShapeDtypeStruct((t, D_CONV), BF16),
        ],
        scratch_shapes=[pltpu.VMEM((1, LANES), F32)],
        compiler_params=_params("arbitrary", "arbitrary"),
        name="inproj",
    )(x2d, g, w_packed, bf_pad, b_glu, tri)


def _attn_kernel(q_ref, k_ref, v_ref, c_ref, o_ref, ka_ref, va_ref, qa_ref, m_ref, acc_ref, s_ref, p_ref, alpha_ref):
    i = pl.program_id(1)
    seq = k_ref.shape[0]

    def halves(rows):
        lane = lax.broadcasted_iota(I32, (rows, LANES), 1)
        own = (lane < HEAD_DIM, lane >= HEAD_DIM)
        aug = (lane - HEAD_DIM, lane)
        return own, aug

    def split3_f32(c):
        return [p.astype(F32) for p in _split3(c)]

    @pl.when(i == 0)
    def _():
        own, aug = halves(seq)
        c = c_ref[...]
        for p in range(HEAD_PAIRS):
            k = k_ref[:, p * LANES:(p + 1) * LANES].astype(F32)
            v = v_ref[:, p * LANES:(p + 1) * LANES].astype(F32)
            for h in range(2):
                a = aug[h]
                hi, mid, lo = split3_f32(c[:, 2 * p + h:2 * p + h + 1])
                extra = jnp.where(a < 3, 1.0, jnp.where(a == 3, -hi, jnp.where(a == 4, -mid, jnp.where(a == 5, -lo, 0.0))))
                ka_ref[2 * p + h] = jnp.where(own[h], k, extra).astype(BF16)
                va_ref[2 * p + h] = jnp.where(own[h], v, 1.0).astype(BF16)

    row0 = pl.multiple_of(i * TQ, TQ)
    own, aug = halves(TQ)
    cq = c_ref[pl.ds(row0, TQ), :]
    for p in range(HEAD_PAIRS):
        q = q_ref[:, p * LANES:(p + 1) * LANES].astype(F32) * (HEAD_DIM ** -0.5 * LOG2E)
        for h in range(2):
            a = aug[h]
            hi, mid, lo = split3_f32(cq[:, 2 * p + h:2 * p + h + 1])
            extra = jnp.where(a == 0, hi, jnp.where(a == 1, mid, jnp.where(a == 2, lo, jnp.where(a < 6, 1.0, 0.0))))
            qa_ref[2 * p + h] = jnp.where(own[h], q, extra).astype(BF16)
    m_ref[...] = jnp.full(m_ref.shape, NEG_BIG, F32)
    acc_ref[...] = jnp.zeros_like(acc_ref)

    nt_dims = (((1,), (1,)), ((), ()))

    def chunk(j, masked):
        k0 = pl.multiple_of(j * TQ, TQ)
        for hh in range(ATTN_HEADS):
            s_ref[hh] = lax.dot_general(qa_ref[hh], ka_ref[hh, pl.ds(k0, TQ), :], nt_dims,
                                        preferred_element_type=F32)
        for hh in range(ATTN_HEADS):
            s = s_ref[hh]
            if masked:
                r = lax.broadcasted_iota(I32, (TQ, TQ), 0)
                cc = lax.broadcasted_iota(I32, (TQ, TQ), 1)
                s = jnp.where(r >= cc, s, NEG_BIG)
            m_old = m_ref[hh]
            m_new = jnp.maximum(m_old, jnp.max(s, axis=-1, keepdims=True))
            p_ref[hh] = jnp.concatenate(
                [jnp.exp2(s[:, n * LANES:(n + 1) * LANES] - m_new) for n in range(TQ // LANES)], axis=1).astype(BF16)
            alpha_ref[hh] = jnp.exp2(m_old - m_new)
            m_ref[hh] = m_new
        for hh in range(ATTN_HEADS):
            pv = jnp.dot(p_ref[hh], va_ref[hh, pl.ds(k0, TQ), :], preferred_element_type=F32)
            acc_ref[hh] = alpha_ref[hh] * acc_ref[hh] + pv

    def body(j, carry):
        chunk(j, False)
        return carry

    lax.fori_loop(0, i, body, 0)
    chunk(i, True)
    for p in range(HEAD_PAIRS):
        outs = [acc_ref[hh] / pltpu.roll(acc_ref[hh], HEAD_DIM, 1) for hh in (2 * p, 2 * p + 1)]
        o_ref[:, p * LANES:(p + 1) * LANES] = jnp.where(own[0], outs[0], outs[1]).astype(BF16)


def _fox_attn(q, k, v, c, batch, seq):
    t = batch * seq
    nq = seq // TQ
    return pl.pallas_call(
        _attn_kernel,
        grid=(batch, nq),
        in_specs=[
            pl.BlockSpec((TQ, D_ATTN), lambda b, i: (b * nq + i, 0)),
            pl.BlockSpec((seq, D_ATTN), lambda b, i: (b, 0)),
            pl.BlockSpec((seq, D_ATTN), lambda b, i: (b, 0)),
            pl.BlockSpec((seq, ATTN_HEADS), lambda b, i: (b, 0)),
        ],
        out_specs=pl.BlockSpec((TQ, D_ATTN), lambda b, i: (b * nq + i, 0)),
        out_shape=jax.ShapeDtypeStruct((t, D_ATTN), BF16),
        scratch_shapes=[
            pltpu.VMEM((ATTN_HEADS, seq, LANES), BF16),
            pltpu.VMEM((ATTN_HEADS, seq, LANES), BF16),
            pltpu.VMEM((ATTN_HEADS, TQ, LANES), BF16),
            pltpu.VMEM((ATTN_HEADS, TQ, LANES), F32),
            pltpu.VMEM((ATTN_HEADS, TQ, LANES), F32),
            pltpu.VMEM((ATTN_HEADS, TQ, TQ), F32),
            pltpu.VMEM((ATTN_HEADS, TQ, TQ), BF16),
            pltpu.VMEM((ATTN_HEADS, TQ, LANES), F32),
        ],
        compiler_params=_params("arbitrary", "arbitrary"),
        name="fox_attn",
    )(q, k, v, c)


def _conv_kernel(z_ref, w_ref, b_ref, g_ref, beta_ref, o_ref, buf_ref, sh_ref):
    @pl.when(pl.program_id(1) == 0)
    def _():
        buf_ref[0:CONV_HALO, :] = jnp.zeros((CONV_HALO, D_CONV), F32)

    buf_ref[CONV_HALO:, :] = z_ref[...].astype(F32)
    span = TS_CONV + CONV_HALO - SUBLANES
    for s in range(1, SUBLANES):
        sh_ref[s, 0:span, :] = buf_ref[s:s + span, :]
    first = CONV_HALO - (CONV_WIDTH - 1)
    for r in range(TS_CONV // CONV_ROWS):
        acc = jnp.zeros((CONV_ROWS, D_CONV), F32)
        for j in range(CONV_WIDTH):
            s = (first + j) % SUBLANES
            lo = r * CONV_ROWS + first + j - s
            tap = sh_ref[s, lo:lo + CONV_ROWS, :] if s else buf_ref[lo:lo + CONV_ROWS, :]
            acc = acc + tap * w_ref[j:j + 1, :]
        y = acc + b_ref[...]
        mu = jnp.mean(y, axis=-1, keepdims=True)
        yc = y - mu
        y = yc * lax.rsqrt(jnp.mean(yc * yc, axis=-1, keepdims=True) + EPS) * g_ref[...] + beta_ref[...]
        o_ref[r * CONV_ROWS:(r + 1) * CONV_ROWS, :] = (y * jax.nn.sigmoid(y)).astype(BF16)
    buf_ref[0:CONV_HALO, :] = buf_ref[TS_CONV:TS_CONV + CONV_HALO, :]


def _conv(z, w_dw_pad, b_dw, ln_g, ln_b, batch, seq):
    t = batch * seq
    ns = seq // TS_CONV
    tok = lambda b, i: (b * ns + i, 0)
    const = lambda b, i: (0, 0)
    return pl.pallas_call(
        _conv_kernel,
        grid=(batch, ns),
        in_specs=[
            pl.BlockSpec((TS_CONV, D_CONV), tok),
            pl.BlockSpec((CONV_HALO, D_CONV), const),
            pl.BlockSpec((1, D_CONV), const),
            pl.BlockSpec((1, D_CONV), const),
            pl.BlockSpec((1, D_CONV), const),
        ],
        out_specs=pl.BlockSpec((TS_CONV, D_CONV), tok),
        out_shape=jax.ShapeDtypeStruct((t, D_CONV), BF16),
        scratch_shapes=[pltpu.VMEM((TS_CONV + CONV_HALO, D_CONV), F32),
                        pltpu.VMEM((SUBLANES, TS_CONV + CONV_HALO, D_CONV), F32)],
        compiler_params=_params("arbitrary", "arbitrary"),
        name="conv",
    )(z, w_dw_pad, b_dw, ln_g, ln_b)


def _memkv_kernel(m_ref, g_ref, w_ref, k_ref, v_ref):
    h = _rms(m_ref[...], g_ref[...]).astype(BF16)
    k = jnp.dot(h, w_ref[:, :D_MODEL], preferred_element_type=F32)
    k_ref[...] = (k * (MEM_HEAD_DIM ** -0.5)).astype(BF16)
    v_ref[...] = jnp.dot(h, w_ref[:, D_MODEL:], preferred_element_type=F32).astype(BF16)


def _mem_kv(mem2d, g, w_mkv, batch, n_mem):
    return pl.pallas_call(
        _memkv_kernel,
        grid=(batch,),
        in_specs=[
            pl.BlockSpec((n_mem, D_MODEL), lambda b: (b, 0)),
            pl.BlockSpec((1, D_MODEL), lambda b: (0, 0)),
            pl.BlockSpec((D_MODEL, 2 * D_MODEL), lambda b: (0, 0)),
        ],
        out_specs=[pl.BlockSpec((n_mem, D_MODEL), lambda b: (b, 0))] * 2,
        out_shape=[jax.ShapeDtypeStruct((batch * n_mem, D_MODEL), BF16)] * 2,
        compiler_params=_params("arbitrary"),
        name="mem_kv",
    )(mem2d, g, w_mkv)


def _mid_kernel(x_ref, a_ref, cv_ref, ag_ref, wout_ref, gm_ref, wmq_ref, km_ref, vm_ref, wmo_ref,
                gf_ref, wr_ref, br_ref, tri_ref,
                x2_ref, row_ref, rb_ref, cnt_ref):
    @pl.when(pl.program_id(0) == 0)
    def _():
        cnt_ref[...] = jnp.zeros_like(cnt_ref)

    a = _rms(a_ref[...].astype(F32), ag_ref[...]).astype(BF16)
    mix = jnp.dot(a, wout_ref[:D_ATTN, :], preferred_element_type=F32)
    mix = mix + jnp.dot(cv_ref[...], wout_ref[D_ATTN:, :], preferred_element_type=F32)
    x1 = x_ref[...] + mix

    h2 = _rms(x1, gm_ref[...]).astype(BF16)
    qm = jnp.dot(h2, wmq_ref[...], preferred_element_type=F32).astype(BF16)
    heads = []
    for hh in range(MEM_HEADS):
        sl = slice(hh * MEM_HEAD_DIM, (hh + 1) * MEM_HEAD_DIM)
        s = lax.dot_general(qm[:, sl], km_ref[:, sl], (((1,), (1,)), ((), ())), preferred_element_type=F32)
        p = jnp.exp(s - jnp.max(s, axis=-1, keepdims=True))
        l = jnp.sum(p, axis=-1, keepdims=True)
        o = jnp.dot(p.astype(BF16), vm_ref[:, sl], preferred_element_type=F32) / l
        heads.append(o.astype(BF16))
    o = jnp.concatenate(heads, axis=-1)
    x2 = x1 + jnp.dot(o, wmo_ref[...], preferred_element_type=F32)
    x2_ref[...] = x2

    h3 = _rms(x2, gf_ref[...])
    h_hi = h3.astype(BF16)
    h_lo = (h3 - h_hi.astype(F32)).astype(BF16)
    logits = (jnp.dot(h_hi, wr_ref[0], preferred_element_type=F32)
              + jnp.dot(h_hi, wr_ref[1], preferred_element_type=F32)
              + jnp.dot(h_lo, wr_ref[0], preferred_element_type=F32)) + br_ref[...]
    lane = lax.broadcasted_iota(I32, logits.shape, 1)

    def top(vals):
        m = jnp.max(vals, axis=-1, keepdims=True)
        idx = jnp.min(jnp.where(vals == m, lane, LANES), axis=-1, keepdims=True)
        return m, idx

    gl = jnp.where(lane < N_GROUPS, logits, NEG_BIG)
    gmax, gidx = top(gl)
    g_top = 1.0 / jnp.sum(jnp.exp(gl - gmax), axis=-1, keepdims=True)
    first = ROUTE_OFF + EXPERTS_PER_GROUP * gidx
    el = jnp.where((lane >= first) & (lane < first + EXPERTS_PER_GROUP), logits, NEG_BIG)
    m1, i1 = top(el)
    m2, i2 = top(jnp.where(lane == i1, NEG_BIG, el))
    e2 = jnp.exp(m2 - m1)
    w1 = g_top / (1.0 + e2)
    w2 = g_top * e2 / (1.0 + e2)

    swap = i2 < i1
    a = jnp.minimum(i1, i2) - first
    b = jnp.maximum(i1, i2) - first
    pair = a * EXPERTS_PER_GROUP - ((a * (a + 1)) >> 1) + (b - a - 1)
    bucket = gidx * PAIRS_PER_GROUP + pair
    wa = jnp.where(swap, w2, w1)
    wb = jnp.where(swap, w1, w2)
    row_ref[:, :D_MODEL] = h3
    row_ref[:, D_MODEL:] = jnp.where(lane == 0, wa, jnp.where(lane == 1, wb, 0.0))

    hit = lane == bucket
    cum = cnt_ref[...] + jnp.dot(tri_ref[...], jnp.where(hit, 1.0, 0.0).astype(BF16), preferred_element_type=F32)
    rank = jnp.sum(jnp.where(hit, cum, 0.0), axis=-1, keepdims=True) - 1.0
    cnt_ref[...] = cum[-1:, :]
    l8 = lax.broadcasted_iota(I32, (TM_MID, SUBLANES), 1)
    rb_ref[...] = jnp.where(l8 == 0, bucket.astype(F32), jnp.where(l8 == 1, rank, 0.0))


def _mid(x2d, attn, conv, ag, w_out, gm, w_mq, kmem, vmem, w_mo, gf, w_route, b_route, batch, seq, n_mem):
    t = batch * seq
    per_b = seq // TM_MID
    tri = jnp.tril(jnp.ones((TM_MID, TM_MID), BF16))
    tok = lambda i: (i, 0)
    const = lambda i: (0, 0)
    memb = lambda i: (i // per_b, 0)
    return pl.pallas_call(
        _mid_kernel,
        grid=(t // TM_MID,),
        in_specs=[
            pl.BlockSpec((TM_MID, D_MODEL), tok),
            pl.BlockSpec((TM_MID, D_ATTN), tok),
            pl.BlockSpec((TM_MID, D_CONV), tok),
            pl.BlockSpec((1, D_ATTN), const),
            pl.BlockSpec((D_MODEL, D_MODEL), const),
            pl.BlockSpec((1, D_MODEL), const),
            pl.BlockSpec((D_MODEL, D_MODEL), const),
            pl.BlockSpec((n_mem, D_MODEL), memb),
            pl.BlockSpec((n_mem, D_MODEL), memb),
            pl.BlockSpec((D_MODEL, D_MODEL), const),
            pl.BlockSpec((1, D_MODEL), const),
            pl.BlockSpec((2, D_MODEL, LANES), lambda i: (0, 0, 0)),
            pl.BlockSpec((1, LANES), const),
            pl.BlockSpec((TM_MID, TM_MID), const),
        ],
        out_specs=[
            pl.BlockSpec((TM_MID, D_MODEL), tok),
            pl.BlockSpec((TM_MID, D_ROW), tok),
            pl.BlockSpec((TM_MID, SUBLANES), tok),
            pl.BlockSpec((1, LANES), const),
        ],
        out_shape=[
            jax.ShapeDtypeStruct((t, D_MODEL), F32),
            jax.ShapeDtypeStruct((t, D_ROW), F32),
            jax.ShapeDtypeStruct((t, SUBLANES), F32),
            jax.ShapeDtypeStruct((1, LANES), F32),
        ],
        compiler_params=_params("arbitrary"),
        name="mid",
    )(x2d, attn, conv, ag, w_out, gm, w_mq, kmem, vmem, w_mo, gf, w_route, b_route, tri)


def _bucket_expert_tables():
    ea = np.zeros((1, LANES), np.float32)
    eb = np.zeros((1, LANES), np.float32)
    for g in range(N_GROUPS):
        n = 0
        for a in range(EXPERTS_PER_GROUP):
            for b in range(a + 1, EXPERTS_PER_GROUP):
                ea[0, g * PAIRS_PER_GROUP + n] = g * EXPERTS_PER_GROUP + a
                eb[0, g * PAIRS_PER_GROUP + n] = g * EXPERTS_PER_GROUP + b
                n += 1
    return jnp.asarray(ea), jnp.asarray(eb)


def _plan_kernel(cnt_ref, rb_ref, ea_ref, eb_ref, triu_ref, pos_ref, tinfo_ref):
    hi = lax.Precision.HIGHEST
    cnt = cnt_ref[...]
    padded = jnp.floor((cnt + (TMX - 1)) * (1.0 / TMX)) * TMX
    end = jnp.dot(padded, triu_ref[...], preferred_element_type=F32, precision=hi)
    base = end - padded

    @pl.when(pl.program_id(0) == 0)
    def _():
        n_tiles = tinfo_ref.shape[0]
        lane = lax.broadcasted_iota(I32, (n_tiles, LANES), 1)
        start = lax.broadcasted_iota(I32, (n_tiles, LANES), 0).astype(F32) * TMX
        done = (lane < N_BUCKETS) & (end <= start)
        tb = jnp.sum(jnp.where(done, 1, 0), axis=-1, keepdims=True)
        ea = jnp.sum(jnp.where(lane == tb, ea_ref[...], 0.0), axis=-1, keepdims=True)
        eb = jnp.sum(jnp.where(lane == tb, eb_ref[...], 0.0), axis=-1, keepdims=True)
        used = jnp.sum(jnp.where(lane == N_BUCKETS - 1, end, 0.0), axis=-1, keepdims=True) * (1.0 / TMX)
        l8 = lax.broadcasted_iota(I32, (n_tiles, SUBLANES), 1)
        tinfo_ref[...] = jnp.where(l8 == 0, ea, jnp.where(l8 == 1, eb, jnp.where(l8 == 2, used, 0.0))).astype(I32)

    rb = rb_ref[...]
    lane = lax.broadcasted_iota(I32, (rb.shape[0], LANES), 1)
    hit = lane == rb[:, 0:1].astype(I32)
    dest = jnp.where(hit, base + rb[:, 1:2], 0.0)
    pos = lax.dot_general(jnp.ones((SUBLANES, LANES), F32), dest, (((1,), (1,)), ((), ())),
                          preferred_element_type=F32, precision=hi)
    pos_ref[...] = pos.astype(I32)


def _moe_plan(counts, rb, n_tiles):
    t = rb.shape[0]
    ea, eb = _bucket_expert_tables()
    triu = jnp.triu(jnp.ones((LANES, LANES), F32))
    const = lambda i: (0, 0)
    return pl.pallas_call(
        _plan_kernel,
        grid=(t // TM_PLAN,),
        in_specs=[
            pl.BlockSpec((1, LANES), const),
            pl.BlockSpec((TM_PLAN, SUBLANES), lambda i: (i, 0)),
            pl.BlockSpec((1, LANES), const),
            pl.BlockSpec((1, LANES), const),
            pl.BlockSpec((LANES, LANES), const),
        ],
        out_specs=[
            pl.BlockSpec((SUBLANES, TM_PLAN), lambda i: (0, i)),
            pl.BlockSpec((n_tiles, SUBLANES), const),
        ],
        out_shape=[
            jax.ShapeDtypeStruct((SUBLANES, t), I32),
            jax.ShapeDtypeStruct((n_tiles, SUBLANES), I32),
        ],
        compiler_params=_params("arbitrary"),
        name="moe_plan",
    )(counts, rb, ea, eb, triu)


def _dispatch_kernel(pos_ref, rows_ref, init_hbm, xs_hbm, sem):
    del init_hbm
    t0 = pl.program_id(0) * TM_DISP

    for r in range(TM_DISP):
        pltpu.make_async_copy(rows_ref.at[pl.ds(r, 1), :], xs_hbm.at[pl.ds(pos_ref[t0 + r], 1), :],
                              sem).start(priority=r % 2)
    pltpu.make_async_copy(rows_ref, xs_hbm.at[pl.ds(0, TM_DISP), :], sem).wait()


def _moe_dispatch(pos, rows, n_rows):
    t = rows.shape[0]
    init = jnp.zeros((n_rows, D_ROW), F32)
    return pl.pallas_call(
        _dispatch_kernel,
        grid_spec=pltpu.PrefetchScalarGridSpec(
            num_scalar_prefetch=1,
            grid=(t // TM_DISP,),
            in_specs=[pl.BlockSpec((TM_DISP, D_ROW), lambda i, pos: (i, 0)), pl.BlockSpec(memory_space=pl.ANY)],
            out_specs=pl.BlockSpec(memory_space=pl.ANY),
            scratch_shapes=[pltpu.SemaphoreType.DMA],
        ),
        out_shape=jax.ShapeDtypeStruct((n_rows, D_ROW), F32),
        input_output_aliases={2: 0},
        compiler_params=_params("arbitrary"),
        name="moe_dispatch",
    )(pos, rows, init)


def _experts_kernel(ea_ref, eb_ref, used_ref, xs_ref, wga_ref, wua_ref, wda_ref, wgb_ref, wub_ref, wdb_ref, ys_ref):
    del ea_ref, eb_ref

    @pl.when(pl.program_id(0) < used_ref[0])
    def _():
        x = xs_ref[:, :D_MODEL].astype(BF16)
        gates = xs_ref[:, D_MODEL:]
        y = None
        for col, wg_ref, wu_ref, wd_ref in ((0, wga_ref, wua_ref, wda_ref), (1, wgb_ref, wub_ref, wdb_ref)):
            hg = jnp.dot(x, wg_ref[...], preferred_element_type=F32)
            hu = jnp.dot(x, wu_ref[...], preferred_element_type=F32)
            act = (hg * jax.nn.sigmoid(hg) * hu * gates[:, col:col + 1]).astype(BF16)
            part = jnp.dot(act, wd_ref[...], preferred_element_type=F32)
            y = part if y is None else y + part
        ys_ref[...] = y

    @pl.when(pl.program_id(0) >= used_ref[0])
    def _():
        ys_ref[...] = jnp.zeros_like(ys_ref)


def _moe_experts(tile_ea, tile_eb, used, xs, w_gate, w_up, w_down):
    n_rows = xs.shape[0]
    n_tiles = n_rows // TMX
    last = lambda r, used: jnp.maximum(jnp.minimum(r, used[0] - 1), 0)
    row = lambda r, ea, eb, used: (last(r, used), 0)
    wa = lambda r, ea, eb, used: (ea[last(r, used)], 0, 0)
    wb = lambda r, ea, eb, used: (eb[last(r, used)], 0, 0)
    return pl.pallas_call(
        _experts_kernel,
        grid_spec=pltpu.PrefetchScalarGridSpec(
            num_scalar_prefetch=3,
            grid=(n_tiles,),
            in_specs=[
                pl.BlockSpec((TMX, D_ROW), row),
                pl.BlockSpec((None, D_MODEL, D_EXPERT), wa),
                pl.BlockSpec((None, D_MODEL, D_EXPERT), wa),
                pl.BlockSpec((None, D_EXPERT, D_MODEL), wa),
                pl.BlockSpec((None, D_MODEL, D_EXPERT), wb),
                pl.BlockSpec((None, D_MODEL, D_EXPERT), wb),
                pl.BlockSpec((None, D_EXPERT, D_MODEL), wb),
            ],
            out_specs=pl.BlockSpec((TMX, D_MODEL), lambda r, ea, eb, used: (r, 0)),
        ),
        out_shape=jax.ShapeDtypeStruct((n_rows, D_MODEL), F32),
        compiler_params=_params("arbitrary"),
        name="moe_experts",
    )(tile_ea, tile_eb, used, xs, w_gate, w_up, w_down, w_gate, w_up, w_down)


def _combine_kernel(pos_ref, x2_ref, fg_ref, ys_hbm, o_ref, ybuf, sem):
    i = pl.program_id(0)
    n_tiles = pl.num_programs(0) - 1
    slot = lax.rem(i, 2)

    @pl.when(i < n_tiles)
    def _():
        t0 = i * TM_COMB
        for r in range(TM_COMB):
            pltpu.make_async_copy(ys_hbm.at[pl.ds(pos_ref[t0 + r], 1), :], ybuf.at[slot, pl.ds(r, 1), :],
                                  sem.at[slot]).start(priority=r % 2)

    @pl.when(i > 0)
    def _():
        prev = 1 - slot
        pltpu.make_async_copy(ys_hbm.at[pl.ds(0, TM_COMB), :], ybuf.at[prev], sem.at[prev]).wait()
        o_ref[...] = _rms(x2_ref[...] + ybuf[prev], fg_ref[...])


def _moe_combine(pos, x2, final_g, ys):
    t = x2.shape[0]
    done = lambda i, pos: (jnp.maximum(i - 1, 0), 0)
    return pl.pallas_call(
        _combine_kernel,
        grid_spec=pltpu.PrefetchScalarGridSpec(
            num_scalar_prefetch=1,
            grid=(t // TM_COMB + 1,),
            in_specs=[
                pl.BlockSpec((TM_COMB, D_MODEL), done),
                pl.BlockSpec((1, D_MODEL), lambda i, pos: (0, 0)),
                pl.BlockSpec(memory_space=pl.ANY),
            ],
            out_specs=pl.BlockSpec((TM_COMB, D_MODEL), done),
            scratch_shapes=[pltpu.VMEM((2, TM_COMB, D_MODEL), F32), pltpu.SemaphoreType.DMA((2,))],
        ),
        out_shape=jax.ShapeDtypeStruct((t, D_MODEL), F32),
        compiler_params=_params("arbitrary"),
        name="moe_combine",
    )(pos, x2, final_g, ys)


def kernel(x, mem, norm_mix_g, w_in, b_forget, b_glu, w_dw, b_dw, conv_ln_g, conv_ln_b, attn_out_g, w_out,
           norm_mem_g, mem_norm_g, w_mq, w_mkv, w_mo, norm_ffn_g, w_route_group, b_route_group,
           w_route_expert, b_route_expert, w_gate, w_up, w_down, final_g):
    batch, seq, _ = x.shape
    n_mem = mem.shape[1]
    assert norm_mix_g.shape[0] == 1, "single-layer block"
    t = batch * seq
    xs = x.reshape(t, D_MODEL)
    mem2d = mem.reshape(batch * n_mem, D_MODEL)
    row = lambda v: v.reshape(1, -1)

    wi = w_in[0]
    w_packed = jnp.concatenate(
        [wi[:, :3 * D_ATTN], jnp.pad(wi[:, 3 * D_ATTN:3 * D_ATTN + ATTN_HEADS], ((0, 0), (0, LANES - ATTN_HEADS))),
         wi[:, 3 * D_ATTN + ATTN_HEADS:]], axis=1).astype(BF16)
    bf_pad = jnp.pad(b_forget[0], (0, LANES - ATTN_HEADS)).reshape(1, LANES)
    q, k, v, c, z = _inproj(xs, row(norm_mix_g[0]), w_packed, bf_pad, row(b_glu[0]), batch, seq)
    attn = _fox_attn(q, k, v, c, batch, seq)
    w_dw_pad = jnp.pad(w_dw[0], ((0, CONV_HALO - CONV_WIDTH), (0, 0)))
    conv = _conv(z, w_dw_pad, row(b_dw[0]), row(conv_ln_g[0]), row(conv_ln_b[0]), batch, seq)
    kmem, vmem = _mem_kv(mem2d, row(mem_norm_g[0]), w_mkv[0].astype(BF16), batch, n_mem)

    w_route = jnp.concatenate(
        [w_route_group[0], w_route_expert[0].transpose(1, 0, 2).reshape(D_MODEL, N_EXPERTS)], axis=1)
    w_route = jnp.pad(w_route, ((0, 0), (0, LANES - N_GROUPS - N_EXPERTS)))
    w_route_hi = w_route.astype(BF16)
    w_route = jnp.stack([w_route_hi, (w_route - w_route_hi.astype(F32)).astype(BF16)])
    b_route = jnp.pad(jnp.concatenate([b_route_group[0], b_route_expert[0].reshape(-1)]),
                      (0, LANES - N_GROUPS - N_EXPERTS)).reshape(1, LANES)
    x2, rows, rb, counts = _mid(xs, attn, conv, row(attn_out_g[0]), w_out[0].astype(BF16), row(norm_mem_g[0]),
                                w_mq[0].astype(BF16), kmem, vmem, w_mo[0].astype(BF16), row(norm_ffn_g[0]),
                                w_route, b_route, batch, seq, n_mem)

    n_tiles = t // TMX + N_BUCKETS
    pos8, tinfo = _moe_plan(counts, rb, n_tiles)
    pos = pos8[0]
    sorted_rows = _moe_dispatch(pos, rows, n_tiles * TMX)
    ys = _moe_experts(tinfo[:, 0], tinfo[:, 1], tinfo[:1, 2], sorted_rows,
                      w_gate[0].astype(BF16), w_up[0].astype(BF16), w_down[0].astype(BF16))
    out = _moe_combine(pos, x2, row(final_g), ys)
    return out.reshape(batch, seq, D_MODEL)
```

```python
import functools

import jax
import jax.numpy as jnp
import numpy as np
from jax import lax
from jax.experimental import pallas as pl
from jax.experimental.pallas import tpu as pltpu

F32 = jnp.float32
BF16 = jnp.bfloat16
I32 = jnp.int32

D_MODEL = 1024
ATTN_HEADS = 8
HEAD_DIM = 64
D_ATTN = ATTN_HEADS * HEAD_DIM
D_CONV = D_MODEL - D_ATTN
CONV_WIDTH = 31
MEM_HEADS = 4
MEM_HEAD_DIM = D_MODEL // MEM_HEADS
N_GROUPS = 4
EXPERTS_PER_GROUP = 8
N_EXPERTS = N_GROUPS * EXPERTS_PER_GROUP
D_EXPERT = D_MODEL // 2
EPS = 1e-6

LANES = 128
SUBLANES = 8
HEAD_PAIRS = ATTN_HEADS // 2
COL_Q = 0
COL_K = COL_Q + D_ATTN
COL_V = COL_K + D_ATTN
COL_F = COL_V + D_ATTN
COL_C = COL_F + LANES
D_IN_PACKED = COL_C + 2 * D_CONV
ROUTE_OFF = N_GROUPS
PAIRS_PER_GROUP = EXPERTS_PER_GROUP * (EXPERTS_PER_GROUP - 1) // 2
N_BUCKETS = N_GROUPS * PAIRS_PER_GROUP
D_ROW = D_MODEL + LANES
NEG_BIG = -1e30
LOG2E = 1.4426950408889634
VMEM_LIMIT = 48 * 1024 * 1024

TM_IN = 512
TQ = 256
CONV_HALO = 32
CONV_ROWS = 64
TM_MID = 512
TM_PLAN = 2048
TM_DISP = 512
TMX = 256
TM_COMB = 512

assert N_BUCKETS <= LANES


def _rms(x, g):
    return x * lax.rsqrt(jnp.mean(x * x, axis=-1, keepdims=True) + EPS) * g


def _split3(c):
    hi = c.astype(BF16)
    r = c - hi.astype(F32)
    mid = r.astype(BF16)
    lo = (r - mid.astype(F32)).astype(BF16)
    return hi, mid, lo


def _params(*sem):
    return pltpu.CompilerParams(dimension_semantics=sem, vmem_limit_bytes=VMEM_LIMIT)


def _inproj_conv_kernel(x_ref, g_ref, w_ref, bf_ref, bglu_ref, tri_ref, wdw_ref, bdw_ref, lng_ref, lnb_ref,
                        q_ref, k_ref, v_ref, c_ref, o_ref, carry_ref, z_ref, buf_ref, sh_ref, *, tiles_per_seq):
    g = pl.program_id(0)
    n_tiles = pl.num_programs(0) - 1
    slot = lax.rem(g, 2)

    @pl.when(g == 0)
    def _():
        z_ref[...] = jnp.zeros_like(z_ref)
        buf_ref[...] = jnp.zeros_like(buf_ref)

    @pl.when(lax.rem(g, tiles_per_seq) == 0)
    def _():
        carry_ref[...] = jnp.zeros_like(carry_ref)

    seq_start = lax.rem(g - 1, tiles_per_seq) == 0
    buf_ref[0:CONV_HALO, :] = jnp.where(seq_start, 0.0, buf_ref[TM_IN:TM_IN + CONV_HALO, :])
    buf_ref[CONV_HALO:, :] = z_ref[1 - slot].astype(F32)
    span = TM_IN + CONV_HALO - SUBLANES
    for s in range(1, SUBLANES):
        sh_ref[s, 0:span, :] = buf_ref[s:s + span, :]
    first = CONV_HALO - (CONV_WIDTH - 1)

    def conv_rows(r):
        acc = jnp.zeros((CONV_ROWS, D_CONV), F32)
        for j in range(CONV_WIDTH):
            s = (first + j) % SUBLANES
            lo = r * CONV_ROWS + first + j - s
            tap = sh_ref[s, lo:lo + CONV_ROWS, :] if s else buf_ref[lo:lo + CONV_ROWS, :]
            w = wdw_ref[j * SUBLANES:(j + 1) * SUBLANES, :]
            acc = acc + tap * jnp.concatenate([w] * (CONV_ROWS // SUBLANES), axis=0)
        y = acc + bdw_ref[...]
        mu = jnp.mean(y, axis=-1, keepdims=True)
        yc = y - mu
        y = yc * lax.rsqrt(jnp.mean(yc * yc, axis=-1, keepdims=True) + EPS) * lng_ref[...] + lnb_ref[...]
        o_ref[r * CONV_ROWS:(r + 1) * CONV_ROWS, :] = (y * jax.nn.sigmoid(y)).astype(BF16)

    for r in range(TM_IN // CONV_ROWS):
        conv_rows(r)

    h = _rms(x_ref[...], g_ref[...]).astype(BF16)
    q_ref[...] = jnp.dot(h, w_ref[:, COL_Q:COL_K], preferred_element_type=F32).astype(BF16)
    k_ref[...] = jnp.dot(h, w_ref[:, COL_K:COL_V], preferred_element_type=F32).astype(BF16)
    v_ref[...] = jnp.dot(h, w_ref[:, COL_V:COL_F], preferred_element_type=F32).astype(BF16)

    fl = jnp.dot(h, w_ref[:, COL_F:COL_C], preferred_element_type=F32) + bf_ref[...]
    lf = jnp.minimum(fl, 0.0) - jnp.log(1.0 + jnp.exp(-jnp.abs(fl)))
    tri = tri_ref[...]
    c = carry_ref[...]
    for part in _split3(lf):
        c = c + jnp.dot(tri, part, preferred_element_type=F32)

    u = jnp.dot(h, w_ref[:, COL_C:D_IN_PACKED], preferred_element_type=F32) + bglu_ref[...]
    z_ref[slot] = (u[:, :D_CONV] * jax.nn.sigmoid(u[:, D_CONV:])).astype(BF16)

    @pl.when(g < n_tiles)
    def _():
        carry_ref[...] = c[-1:, :]
        c_ref[...] = c[:, :ATTN_HEADS] * LOG2E

def _inproj_conv(x2d, g, w_packed, bf_pad, b_glu, w_dw_tiles, b_dw, ln_g, ln_b, batch, seq):
    t = batch * seq
    nt = seq // TM_IN
    n_tiles = batch * nt
    tri = jnp.tril(jnp.ones((TM_IN, TM_IN), BF16))
    cur = lambda s: (jnp.minimum(s, n_tiles - 1), 0)
    done = lambda s: (jnp.maximum(s - 1, 0), 0)
    const = lambda s: (0, 0)
    return pl.pallas_call(
        functools.partial(_inproj_conv_kernel, tiles_per_seq=nt),
        grid=(n_tiles + 1,),
        in_specs=[
            pl.BlockSpec((TM_IN, D_MODEL), cur),
            pl.BlockSpec((1, D_MODEL), const),
            pl.BlockSpec((D_MODEL, D_IN_PACKED), const),
            pl.BlockSpec((1, LANES), const),
            pl.BlockSpec((1, 2 * D_CONV), const),
            pl.BlockSpec((TM_IN, TM_IN), const),
            pl.BlockSpec((CONV_WIDTH * SUBLANES, D_CONV), const),
            pl.BlockSpec((1, D_CONV), const),
            pl.BlockSpec((1, D_CONV), const),
            pl.BlockSpec((1, D_CONV), const),
        ],
        out_specs=[
            pl.BlockSpec((TM_IN, D_ATTN), cur),
            pl.BlockSpec((TM_IN, D_ATTN), cur),
            pl.BlockSpec((TM_IN, D_ATTN), cur),
            pl.BlockSpec((TM_IN, ATTN_HEADS), cur),
            pl.BlockSpec((TM_IN, D_CONV), done),
        ],
        out_shape=[
            jax.ShapeDtypeStruct((t, D_ATTN), BF16),
            jax.ShapeDtypeStruct((t, D_ATTN), BF16),
            jax.ShapeDtypeStruct((t, D_ATTN), BF16),
            jax.ShapeDtypeStruct((t, ATTN_HEADS), F32),
            jax.ShapeDtypeStruct((t, D_CONV), BF16),
        ],
        scratch_shapes=[
            pltpu.VMEM((1, LANES), F32),
            pltpu.VMEM((2, TM_IN, D_CONV), BF16),
            pltpu.VMEM((TM_IN + CONV_HALO, D_CONV), F32),
            pltpu.VMEM((SUBLANES, TM_IN + CONV_HALO, D_CONV), F32),
        ],
        compiler_params=_params("arbitrary"),
        name="inproj_conv",
    )(x2d, g, w_packed, bf_pad, b_glu, tri, w_dw_tiles, b_dw, ln_g, ln_b)


def _attn_kernel(q_ref, k_ref, v_ref, c_ref, o_ref, ka_ref, va_ref, qa_ref, m_ref, acc_ref, s_ref, p_ref, alpha_ref):
    i = pl.program_id(1)
    seq = k_ref.shape[0]

    def halves(rows):
        lane = lax.broadcasted_iota(I32, (rows, LANES), 1)
        own = (lane < HEAD_DIM, lane >= HEAD_DIM)
        aug = (lane - HEAD_DIM, lane)
        return own, aug

    def split3_f32(c):
        return [p.astype(F32) for p in _split3(c)]

    @pl.when(i == 0)
    def _():
        own, aug = halves(seq)
        c = c_ref[...]
        for p in range(HEAD_PAIRS):
            k = k_ref[:, p * LANES:(p + 1) * LANES].astype(F32)
            v = v_ref[:, p * LANES:(p + 1) * LANES].astype(F32)
            for h in range(2):
                a = aug[h]
                hi, mid, lo = split3_f32(c[:, 2 * p + h:2 * p + h + 1])
                extra = jnp.where(a < 3, 1.0, jnp.where(a == 3, -hi, jnp.where(a == 4, -mid, jnp.where(a == 5, -lo, 0.0))))
                ka_ref[2 * p + h] = jnp.where(own[h], k, extra).astype(BF16)
                va_ref[2 * p + h] = jnp.where(own[h], v, 1.0).astype(BF16)

    row0 = pl.multiple_of(i * TQ, TQ)
    own, aug = halves(TQ)
    cq = c_ref[pl.ds(row0, TQ), :]
    for p in range(HEAD_PAIRS):
        q = q_ref[:, p * LANES:(p + 1) * LANES].astype(F32) * (HEAD_DIM ** -0.5 * LOG2E)
        for h in range(2):
            a = aug[h]
            hi, mid, lo = split3_f32(cq[:, 2 * p + h:2 * p + h + 1])
            extra = jnp.where(a == 0, hi, jnp.where(a == 1, mid, jnp.where(a == 2, lo, jnp.where(a < 6, 1.0, 0.0))))
            qa_ref[2 * p + h] = jnp.where(own[h], q, extra).astype(BF16)
    m_ref[...] = jnp.full(m_ref.shape, NEG_BIG, F32)
    acc_ref[...] = jnp.zeros_like(acc_ref)

    nt_dims = (((1,), (1,)), ((), ()))

    def chunk(j, masked):
        k0 = pl.multiple_of(j * TQ, TQ)
        for hh in range(ATTN_HEADS):
            s_ref[hh] = lax.dot_general(qa_ref[hh], ka_ref[hh, pl.ds(k0, TQ), :], nt_dims,
                                        preferred_element_type=F32)
        for hh in range(ATTN_HEADS):
            s = s_ref[hh]
            if masked:
                r = lax.broadcasted_iota(I32, (TQ, TQ), 0)
                cc = lax.broadcasted_iota(I32, (TQ, TQ), 1)
                s = jnp.where(r >= cc, s, NEG_BIG)
            m_old = m_ref[hh]
            m_new = jnp.maximum(m_old, jnp.max(s, axis=-1, keepdims=True))
            p_ref[hh] = jnp.concatenate(
                [jnp.exp2(s[:, n * LANES:(n + 1) * LANES] - m_new) for n in range(TQ // LANES)], axis=1).astype(BF16)
            alpha_ref[hh] = jnp.exp2(m_old - m_new)
            m_ref[hh] = m_new
        for hh in range(ATTN_HEADS):
            pv = jnp.dot(p_ref[hh], va_ref[hh, pl.ds(k0, TQ), :], preferred_element_type=F32)
            acc_ref[hh] = alpha_ref[hh] * acc_ref[hh] + pv

    def body(j, carry):
        chunk(j, False)
        return carry

    lax.fori_loop(0, i, body, 0)
    chunk(i, True)
    for p in range(HEAD_PAIRS):
        outs = [acc_ref[hh] / pltpu.roll(acc_ref[hh], HEAD_DIM, 1) for hh in (2 * p, 2 * p + 1)]
        o_ref[:, p * LANES:(p + 1) * LANES] = jnp.where(own[0], outs[0], outs[1]).astype(BF16)


def _fox_attn(q, k, v, c, batch, seq):
    t = batch * seq
    nq = seq // TQ
    return pl.pallas_call(
        _attn_kernel,
        grid=(batch, nq),
        in_specs=[
            pl.BlockSpec((TQ, D_ATTN), lambda b, i: (b * nq + i, 0)),
            pl.BlockSpec((seq, D_ATTN), lambda b, i: (b, 0)),
            pl.BlockSpec((seq, D_ATTN), lambda b, i: (b, 0)),
            pl.BlockSpec((seq, ATTN_HEADS), lambda b, i: (b, 0)),
        ],
        out_specs=pl.BlockSpec((TQ, D_ATTN), lambda b, i: (b * nq + i, 0)),
        out_shape=jax.ShapeDtypeStruct((t, D_ATTN), BF16),
        scratch_shapes=[
            pltpu.VMEM((ATTN_HEADS, seq, LANES), BF16),
            pltpu.VMEM((ATTN_HEADS, seq, LANES), BF16),
            pltpu.VMEM((ATTN_HEADS, TQ, LANES), BF16),
            pltpu.VMEM((ATTN_HEADS, TQ, LANES), F32),
            pltpu.VMEM((ATTN_HEADS, TQ, LANES), F32),
            pltpu.VMEM((ATTN_HEADS, TQ, TQ), F32),
            pltpu.VMEM((ATTN_HEADS, TQ, TQ), BF16),
            pltpu.VMEM((ATTN_HEADS, TQ, LANES), F32),
        ],
        compiler_params=_params("arbitrary", "arbitrary"),
        name="fox_attn",
    )(q, k, v, c)


def _memkv_kernel(m_ref, g_ref, w_ref, k_ref, v_ref):
    h = _rms(m_ref[...], g_ref[...]).astype(BF16)
    k = jnp.dot(h, w_ref[:, :D_MODEL], preferred_element_type=F32)
    k_ref[...] = (k * (MEM_HEAD_DIM ** -0.5)).astype(BF16)
    v_ref[...] = jnp.dot(h, w_ref[:, D_MODEL:], preferred_element_type=F32).astype(BF16)


def _mem_kv(mem2d, g, w_mkv, batch, n_mem):
    return pl.pallas_call(
        _memkv_kernel,
        grid=(batch,),
        in_specs=[
            pl.BlockSpec((n_mem, D_MODEL), lambda b: (b, 0)),
            pl.BlockSpec((1, D_MODEL), lambda b: (0, 0)),
            pl.BlockSpec((D_MODEL, 2 * D_MODEL), lambda b: (0, 0)),
        ],
        out_specs=[pl.BlockSpec((n_mem, D_MODEL), lambda b: (b, 0))] * 2,
        out_shape=[jax.ShapeDtypeStruct((batch * n_mem, D_MODEL), BF16)] * 2,
        compiler_params=_params("arbitrary"),
        name="mem_kv",
    )(mem2d, g, w_mkv)


def _mid_kernel(x_ref, a_ref, cv_ref, ag_ref, wout_ref, gm_ref, wmq_ref, km_ref, vm_ref, wmo_ref,
                gf_ref, wr_ref, br_ref, tri_ref,
                x2_ref, row_ref, rb_ref, cnt_ref):
    @pl.when(pl.program_id(0) == 0)
    def _():
        cnt_ref[...] = jnp.zeros_like(cnt_ref)

    a = _rms(a_ref[...].astype(F32), ag_ref[...]).astype(BF16)
    mix = jnp.dot(a, wout_ref[:D_ATTN, :], preferred_element_type=F32)
    mix = mix + jnp.dot(cv_ref[...], wout_ref[D_ATTN:, :], preferred_element_type=F32)
    x1 = x_ref[...] + mix

    h2 = _rms(x1, gm_ref[...]).astype(BF16)
    qm = jnp.dot(h2, wmq_ref[...], preferred_element_type=F32).astype(BF16)
    heads = []
    for hh in range(MEM_HEADS):
        sl = slice(hh * MEM_HEAD_DIM, (hh + 1) * MEM_HEAD_DIM)
        s = lax.dot_general(qm[:, sl], km_ref[:, sl], (((1,), (1,)), ((), ())), preferred_element_type=F32)
        p = jnp.exp(s - jnp.max(s, axis=-1, keepdims=True))
        l = jnp.sum(p, axis=-1, keepdims=True)
        o = jnp.dot(p.astype(BF16), vm_ref[:, sl], preferred_element_type=F32) / l
        heads.append(o.astype(BF16))
    o = jnp.concatenate(heads, axis=-1)
    x2 = x1 + jnp.dot(o, wmo_ref[...], preferred_element_type=F32)
    x2_ref[...] = x2

    h3 = _rms(x2, gf_ref[...])
    h_hi = h3.astype(BF16)
    h_lo = (h3 - h_hi.astype(F32)).astype(BF16)
    logits = (jnp.dot(h_hi, wr_ref[0], preferred_element_type=F32)
              + jnp.dot(h_hi, wr_ref[1], preferred_element_type=F32)
              + jnp.dot(h_lo, wr_ref[0], preferred_element_type=F32)) + br_ref[...]
    lane = lax.broadcasted_iota(I32, logits.shape, 1)

    def top(vals):
        m = jnp.max(vals, axis=-1, keepdims=True)
        idx = jnp.min(jnp.where(vals == m, lane, LANES), axis=-1, keepdims=True)
        return m, idx

    gl = jnp.where(lane < N_GROUPS, logits, NEG_BIG)
    gmax, gidx = top(gl)
    g_top = 1.0 / jnp.sum(jnp.exp(gl - gmax), axis=-1, keepdims=True)
    first = ROUTE_OFF + EXPERTS_PER_GROUP * gidx
    el = jnp.where((lane >= first) & (lane < first + EXPERTS_PER_GROUP), logits, NEG_BIG)
    m1, i1 = top(el)
    m2, i2 = top(jnp.where(lane == i1, NEG_BIG, el))
    e2 = jnp.exp(m2 - m1)
    w1 = g_top / (1.0 + e2)
    w2 = g_top * e2 / (1.0 + e2)

    swap = i2 < i1
    a = jnp.minimum(i1, i2) - first
    b = jnp.maximum(i1, i2) - first
    pair = a * EXPERTS_PER_GROUP - ((a * (a + 1)) >> 1) + (b - a - 1)
    bucket = gidx * PAIRS_PER_GROUP + pair
    wa = jnp.where(swap, w2, w1)
    wb = jnp.where(swap, w1, w2)
    row_ref[:, :D_MODEL] = h3
    row_ref[:, D_MODEL:] = jnp.where(lane == 0, wa, jnp.where(lane == 1, wb, 0.0))

    hit = lane == bucket
    cum = cnt_ref[...] + jnp.dot(tri_ref[...], jnp.where(hit, 1.0, 0.0).astype(BF16), preferred_element_type=F32)
    rank = jnp.sum(jnp.where(hit, cum, 0.0), axis=-1, keepdims=True) - 1.0
    cnt_ref[...] = cum[-1:, :]
    l8 = lax.broadcasted_iota(I32, (TM_MID, SUBLANES), 1)
    rb_ref[...] = jnp.where(l8 == 0, bucket.astype(F32), jnp.where(l8 == 1, rank, 0.0))


def _mid(x2d, attn, conv, ag, w_out, gm, w_mq, kmem, vmem, w_mo, gf, w_route, b_route, batch, seq, n_mem):
    t = batch * seq
    per_b = seq // TM_MID
    tri = jnp.tril(jnp.ones((TM_MID, TM_MID), BF16))
    tok = lambda i: (i, 0)
    const = lambda i: (0, 0)
    memb = lambda i: (i // per_b, 0)
    return pl.pallas_call(
        _mid_kernel,
        grid=(t // TM_MID,),
        in_specs=[
            pl.BlockSpec((TM_MID, D_MODEL), tok),
            pl.BlockSpec((TM_MID, D_ATTN), tok),
            pl.BlockSpec((TM_MID, D_CONV), tok),
            pl.BlockSpec((1, D_ATTN), const),
            pl.BlockSpec((D_MODEL, D_MODEL), const),
            pl.BlockSpec((1, D_MODEL), const),
            pl.BlockSpec((D_MODEL, D_MODEL), const),
            pl.BlockSpec((n_mem, D_MODEL), memb),
            pl.BlockSpec((n_mem, D_MODEL), memb),
            pl.BlockSpec((D_MODEL, D_MODEL), const),
            pl.BlockSpec((1, D_MODEL), const),
            pl.BlockSpec((2, D_MODEL, LANES), lambda i: (0, 0, 0)),
            pl.BlockSpec((1, LANES), const),
            pl.BlockSpec((TM_MID, TM_MID), const),
        ],
        out_specs=[
            pl.BlockSpec((TM_MID, D_MODEL), tok),
            pl.BlockSpec((TM_MID, D_ROW), tok),
            pl.BlockSpec((TM_MID, SUBLANES), tok),
            pl.BlockSpec((1, LANES), const),
        ],
        out_shape=[
            jax.ShapeDtypeStruct((t, D_MODEL), F32),
            jax.ShapeDtypeStruct((t, D_ROW), F32),
            jax.ShapeDtypeStruct((t, SUBLANES), F32),
            jax.ShapeDtypeStruct((1, LANES), F32),
        ],
        compiler_params=_params("arbitrary"),
        name="mid",
    )(x2d, attn, conv, ag, w_out, gm, w_mq, kmem, vmem, w_mo, gf, w_route, b_route, tri)


def _bucket_expert_tables():
    ea = np.zeros((1, LANES), np.float32)
    eb = np.zeros((1, LANES), np.float32)
    for g in range(N_GROUPS):
        n = 0
        for a in range(EXPERTS_PER_GROUP):
            for b in range(a + 1, EXPERTS_PER_GROUP):
                ea[0, g * PAIRS_PER_GROUP + n] = g * EXPERTS_PER_GROUP + a
                eb[0, g * PAIRS_PER_GROUP + n] = g * EXPERTS_PER_GROUP + b
                n += 1
    return jnp.asarray(ea), jnp.asarray(eb)


def _plan_kernel(cnt_ref, rb_ref, ea_ref, eb_ref, triu_ref, pos_ref, tinfo_ref):
    hi = lax.Precision.HIGHEST
    cnt = cnt_ref[...]
    padded = jnp.floor((cnt + (TMX - 1)) * (1.0 / TMX)) * TMX
    end = jnp.dot(padded, triu_ref[...], preferred_element_type=F32, precision=hi)
    base = end - padded

    @pl.when(pl.program_id(0) == 0)
    def _():
        n_tiles = tinfo_ref.shape[0]
        lane = lax.broadcasted_iota(I32, (n_tiles, LANES), 1)
        start = lax.broadcasted_iota(I32, (n_tiles, LANES), 0).astype(F32) * TMX
        done = (lane < N_BUCKETS) & (end <= start)
        tb = jnp.sum(jnp.where(done, 1, 0), axis=-1, keepdims=True)
        ea = jnp.sum(jnp.where(lane == tb, ea_ref[...], 0.0), axis=-1, keepdims=True)
        eb = jnp.sum(jnp.where(lane == tb, eb_ref[...], 0.0), axis=-1, keepdims=True)
        used = jnp.sum(jnp.where(lane == N_BUCKETS - 1, end, 0.0), axis=-1, keepdims=True) * (1.0 / TMX)
        l8 = lax.broadcasted_iota(I32, (n_tiles, SUBLANES), 1)
        tinfo_ref[...] = jnp.where(l8 == 0, ea, jnp.where(l8 == 1, eb, jnp.where(l8 == 2, used, 0.0))).astype(I32)

    rb = rb_ref[...]
    lane = lax.broadcasted_iota(I32, (rb.shape[0], LANES), 1)
    hit = lane == rb[:, 0:1].astype(I32)
    dest = jnp.where(hit, base + rb[:, 1:2], 0.0)
    pos = lax.dot_general(jnp.ones((SUBLANES, LANES), F32), dest, (((1,), (1,)), ((), ())),
                          preferred_element_type=F32, precision=hi)
    pos_ref[...] = pos.astype(I32)


def _moe_plan(counts, rb, n_tiles):
    t = rb.shape[0]
    ea, eb = _bucket_expert_tables()
    triu = jnp.triu(jnp.ones((LANES, LANES), F32))
    const = lambda i: (0, 0)
    return pl.pallas_call(
        _plan_kernel,
        grid=(t // TM_PLAN,),
        in_specs=[
            pl.BlockSpec((1, LANES), const),
            pl.BlockSpec((TM_PLAN, SUBLANES), lambda i: (i, 0)),
            pl.BlockSpec((1, LANES), const),
            pl.BlockSpec((1, LANES), const),
            pl.BlockSpec((LANES, LANES), const),
        ],
        out_specs=[
            pl.BlockSpec((SUBLANES, TM_PLAN), lambda i: (0, i)),
            pl.BlockSpec((n_tiles, SUBLANES), const),
        ],
        out_shape=[
            jax.ShapeDtypeStruct((SUBLANES, t), I32),
            jax.ShapeDtypeStruct((n_tiles, SUBLANES), I32),
        ],
        compiler_params=_params("arbitrary"),
        name="moe_plan",
    )(counts, rb, ea, eb, triu)


def _dispatch_kernel(pos_ref, rows_ref, init_hbm, xs_hbm, sem):
    del init_hbm
    t0 = pl.program_id(0) * TM_DISP

    for r in range(TM_DISP):
        pltpu.make_async_copy(rows_ref.at[pl.ds(r, 1), :], xs_hbm.at[pl.ds(pos_ref[t0 + r], 1), :],
                              sem).start(priority=r % 2)
    pltpu.make_async_copy(rows_ref, xs_hbm.at[pl.ds(0, TM_DISP), :], sem).wait()


def _moe_dispatch(pos, rows, n_rows):
    t = rows.shape[0]
    init = jnp.zeros((n_rows, D_ROW), F32)
    return pl.pallas_call(
        _dispatch_kernel,
        grid_spec=pltpu.PrefetchScalarGridSpec(
            num_scalar_prefetch=1,
            grid=(t // TM_DISP,),
            in_specs=[pl.BlockSpec((TM_DISP, D_ROW), lambda i, pos: (i, 0)), pl.BlockSpec(memory_space=pl.ANY)],
            out_specs=pl.BlockSpec(memory_space=pl.ANY),
            scratch_shapes=[pltpu.SemaphoreType.DMA],
        ),
        out_shape=jax.ShapeDtypeStruct((n_rows, D_ROW), F32),
        input_output_aliases={2: 0},
        compiler_params=_params("arbitrary"),
        name="moe_dispatch",
    )(pos, rows, init)


def _experts_kernel(ea_ref, eb_ref, used_ref, xs_ref, wga_ref, wua_ref, wda_ref, wgb_ref, wub_ref, wdb_ref, ys_ref):
    del ea_ref, eb_ref

    @pl.when(pl.program_id(0) < used_ref[0])
    def _():
        x = xs_ref[:, :D_MODEL].astype(BF16)
        gates = xs_ref[:, D_MODEL:]
        y = None
        for col, wg_ref, wu_ref, wd_ref in ((0, wga_ref, wua_ref, wda_ref), (1, wgb_ref, wub_ref, wdb_ref)):
            hg = jnp.dot(x, wg_ref[...], preferred_element_type=F32)
            hu = jnp.dot(x, wu_ref[...], preferred_element_type=F32)
            act = (hg * jax.nn.sigmoid(hg) * hu * gates[:, col:col + 1]).astype(BF16)
            part = jnp.dot(act, wd_ref[...], preferred_element_type=F32)
            y = part if y is None else y + part
        ys_ref[...] = y

    @pl.when(pl.program_id(0) >= used_ref[0])
    def _():
        ys_ref[...] = jnp.zeros_like(ys_ref)


def _moe_experts(tile_ea, tile_eb, used, xs, w_gate, w_up, w_down):
    n_rows = xs.shape[0]
    n_tiles = n_rows // TMX
    last = lambda r, used: jnp.maximum(jnp.minimum(r, used[0] - 1), 0)
    row = lambda r, ea, eb, used: (last(r, used), 0)
    wa = lambda r, ea, eb, used: (ea[last(r, used)], 0, 0)
    wb = lambda r, ea, eb, used: (eb[last(r, used)], 0, 0)
    return pl.pallas_call(
        _experts_kernel,
        grid_spec=pltpu.PrefetchScalarGridSpec(
            num_scalar_prefetch=3,
            grid=(n_tiles,),
            in_specs=[
                pl.BlockSpec((TMX, D_ROW), row),
                pl.BlockSpec((None, D_MODEL, D_EXPERT), wa),
                pl.BlockSpec((None, D_MODEL, D_EXPERT), wa),
                pl.BlockSpec((None, D_EXPERT, D_MODEL), wa),
                pl.BlockSpec((None, D_MODEL, D_EXPERT), wb),
                pl.BlockSpec((None, D_MODEL, D_EXPERT), wb),
                pl.BlockSpec((None, D_EXPERT, D_MODEL), wb),
            ],
            out_specs=pl.BlockSpec((TMX, D_MODEL), lambda r, ea, eb, used: (r, 0)),
        ),
        out_shape=jax.ShapeDtypeStruct((n_rows, D_MODEL), F32),
        compiler_params=_params("arbitrary"),
        name="moe_experts",
    )(tile_ea, tile_eb, used, xs, w_gate, w_up, w_down, w_gate, w_up, w_down)


def _combine_kernel(pos_ref, x2_ref, fg_ref, ys_hbm, o_ref, ybuf, sem):
    i = pl.program_id(0)
    n_tiles = pl.num_programs(0) - 1
    slot = lax.rem(i, 2)

    @pl.when(i < n_tiles)
    def _():
        t0 = i * TM_COMB
        for r in range(TM_COMB):
            pltpu.make_async_copy(ys_hbm.at[pl.ds(pos_ref[t0 + r], 1), :], ybuf.at[slot, pl.ds(r, 1), :],
                                  sem.at[slot]).start(priority=r % 2)

    @pl.when(i > 0)
    def _():
        prev = 1 - slot
        pltpu.make_async_copy(ys_hbm.at[pl.ds(0, TM_COMB), :], ybuf.at[prev], sem.at[prev]).wait()
        o_ref[...] = _rms(x2_ref[...] + ybuf[prev], fg_ref[...])


def _moe_combine(pos, x2, final_g, ys):
    t = x2.shape[0]
    done = lambda i, pos: (jnp.maximum(i - 1, 0), 0)
    return pl.pallas_call(
        _combine_kernel,
        grid_spec=pltpu.PrefetchScalarGridSpec(
            num_scalar_prefetch=1,
            grid=(t // TM_COMB + 1,),
            in_specs=[
                pl.BlockSpec((TM_COMB, D_MODEL), done),
                pl.BlockSpec((1, D_MODEL), lambda i, pos: (0, 0)),
                pl.BlockSpec(memory_space=pl.ANY),
            ],
            out_specs=pl.BlockSpec((TM_COMB, D_MODEL), done),
            scratch_shapes=[pltpu.VMEM((2, TM_COMB, D_MODEL), F32), pltpu.SemaphoreType.DMA((2,))],
        ),
        out_shape=jax.ShapeDtypeStruct((t, D_MODEL), F32),
        compiler_params=_params("arbitrary"),
        name="moe_combine",
    )(pos, x2, final_g, ys)


def kernel(x, mem, norm_mix_g, w_in, b_forget, b_glu, w_dw, b_dw, conv_ln_g, conv_ln_b, attn_out_g, w_out,
           norm_mem_g, mem_norm_g, w_mq, w_mkv, w_mo, norm_ffn_g, w_route_group, b_route_group,
           w_route_expert, b_route_expert, w_gate, w_up, w_down, final_g):
    batch, seq, _ = x.shape
    n_mem = mem.shape[1]
    assert norm_mix_g.shape[0] == 1, "single-layer block"
    t = batch * seq
    xs = x.reshape(t, D_MODEL)
    mem2d = mem.reshape(batch * n_mem, D_MODEL)
    row = lambda v: v.reshape(1, -1)

    wi = w_in[0]
    w_packed = jnp.concatenate(
        [wi[:, :3 * D_ATTN], jnp.pad(wi[:, 3 * D_ATTN:3 * D_ATTN + ATTN_HEADS], ((0, 0), (0, LANES - ATTN_HEADS))),
         wi[:, 3 * D_ATTN + ATTN_HEADS:]], axis=1).astype(BF16)
    bf_pad = jnp.pad(b_forget[0], (0, LANES - ATTN_HEADS)).reshape(1, LANES)
    w_dw_tiles = jnp.repeat(w_dw[0], SUBLANES, axis=0)
    q, k, v, c, conv = _inproj_conv(xs, row(norm_mix_g[0]), w_packed, bf_pad, row(b_glu[0]), w_dw_tiles,
                                    row(b_dw[0]), row(conv_ln_g[0]), row(conv_ln_b[0]), batch, seq)
    attn = _fox_attn(q, k, v, c, batch, seq)
    kmem, vmem = _mem_kv(mem2d, row(mem_norm_g[0]), w_mkv[0].astype(BF16), batch, n_mem)

    w_route = jnp.concatenate(
        [w_route_group[0], w_route_expert[0].transpose(1, 0, 2).reshape(D_MODEL, N_EXPERTS)], axis=1)
    w_route = jnp.pad(w_route, ((0, 0), (0, LANES - N_GROUPS - N_EXPERTS)))
    w_route_hi = w_route.astype(BF16)
    w_route = jnp.stack([w_route_hi, (w_route - w_route_hi.astype(F32)).astype(BF16)])
    b_route = jnp.pad(jnp.concatenate([b_route_group[0], b_route_expert[0].reshape(-1)]),
                      (0, LANES - N_GROUPS - N_EXPERTS)).reshape(1, LANES)
    x2, rows, rb, counts = _mid(xs, attn, conv, row(attn_out_g[0]), w_out[0].astype(BF16), row(norm_mem_g[0]),
                                w_mq[0].astype(BF16), kmem, vmem, w_mo[0].astype(BF16), row(norm_ffn_g[0]),
                                w_route, b_route, batch, seq, n_mem)

    n_tiles = t // TMX + N_BUCKETS
    pos8, tinfo = _moe_plan(counts, rb, n_tiles)
    pos = pos8[0]
    sorted_rows = _moe_dispatch(pos, rows, n_tiles * TMX)
    ys = _moe_experts(tinfo[:, 0], tinfo[:, 1], tinfo[:1, 2], sorted_rows,
                      w_gate[0].astype(BF16), w_up[0].astype(BF16), w_down[0].astype(BF16))
    out = _moe_combine(pos, x2, row(final_g), ys)
    return out.reshape(batch, seq, D_MODEL)
```
